```python
import math
import jax, jax.numpy as jnp
from jax import lax
import numpy as np

D_MODEL = 1024
BATCH = 8
SEQ = 4096
DEPTH = 1

CHUNK = 64
EPS = 1e-6

RWKV_HEADS = 8
RWKV_HEAD_DIM = 64
RWKV_WIDTH = RWKV_HEADS * RWKV_HEAD_DIM
LORA_W = 64
LORA_A = 64
LORA_G = 128
GN_EPS = 64e-5
RWKV_SPLITS = [RWKV_WIDTH, 2 * RWKV_WIDTH, 3 * RWKV_WIDTH, 3 * RWKV_WIDTH + LORA_W, 3 * RWKV_WIDTH + LORA_W + LORA_A]
N_RWKV_COLS = 3 * RWKV_WIDTH + LORA_W + LORA_A + LORA_G

S5_GROUP = 16
S5_GROUPS = 32
S5_WIDTH = S5_GROUP * S5_GROUPS
S5_STATE = 64
DT_MIN = 1e-3
DT_MAX = 1e-1

N_IN_COLS = N_RWKV_COLS + S5_WIDTH + 2 * D_MODEL

PEER_HEADS = 8
PEER_KEYS = 128
PEER_EXPERTS = PEER_KEYS * PEER_KEYS
PEER_TOPK = 16
PEER_KEY_DIM = 128
PEER_HALF = PEER_KEY_DIM // 2
PEER_BLOCK = 128

kernel_name = 'hybrid_rwkv7_s5_peer_block'


def _rms_norm(x, gain):
    xf = x.astype(jnp.float32)
    y = xf * lax.rsqrt(jnp.mean(xf * xf, axis=-1, keepdims=True) + EPS)
    return (y * gain.astype(jnp.float32)).astype(x.dtype)


def _token_shift(p):
    return jnp.pad(p[:, :-1], ((0, 0), (1, 0), (0, 0)))


def _rwkv7_time_mix(p, mu, w_lora_up, w0, a_lora_up, a0, g_lora_up, k_k, k_a, r_k, ln_w, ln_b, w_o):
    f32 = jnp.float32
    B, S, _ = p.shape
    p = p.astype(f32)
    p = p + (_token_shift(p) - p) * mu.astype(f32)
    r, k, v, xw, xa, xg = jnp.split(p, RWKV_SPLITS, axis=-1)
    w_log = -jax.nn.softplus(-(w0.astype(f32) + jnp.tanh(xw) @ w_lora_up.astype(f32))) - 0.5
    decay = jnp.exp(-jnp.exp(w_log))
    a = jax.nn.sigmoid(a0.astype(f32) + xa @ a_lora_up.astype(f32))
    g = jax.nn.sigmoid(xg) @ g_lora_up.astype(f32)
    kk = k * k_k.astype(f32)
    k = k * (1.0 + (a - 1.0) * k_a.astype(f32))

    def heads(t):
        return t.reshape(B, S, RWKV_HEADS, RWKV_HEAD_DIM)

    r, k, v, decay, a, kk = heads(r), heads(k), heads(v), heads(decay), heads(a), heads(kk)
    kk = kk / jnp.maximum(jnp.sqrt(jnp.sum(kk * kk, axis=-1, keepdims=True)), 1e-12)

    def step(state, inp):
        r_t, k_t, v_t, w_t, kk_t, a_t = inp
        sa = jnp.einsum('bhvk,bhk->bhv', state, -kk_t)
        state = (state * w_t[:, :, None, :]
                 + sa[..., None] * (kk_t * a_t)[:, :, None, :]
                 + v_t[..., None] * k_t[:, :, None, :])
        y_t = jnp.einsum('bhvk,bhk->bhv', state, r_t)
        return state, y_t

    def seq_first(t):
        return jnp.swapaxes(t, 0, 1)

    state0 = jnp.zeros((B, RWKV_HEADS, RWKV_HEAD_DIM, RWKV_HEAD_DIM), f32)
    _, ys = lax.scan(step, state0, (seq_first(r), seq_first(k), seq_first(v),
                                    seq_first(decay), seq_first(kk), seq_first(a)))
    y = jnp.swapaxes(ys, 0, 1)
    mean = jnp.mean(y, axis=-1, keepdims=True)
    var = jnp.mean(jnp.square(y - mean), axis=-1, keepdims=True)
    y = ((y - mean) * lax.rsqrt(var + GN_EPS)).reshape(B, S, RWKV_WIDTH)
    y = y * ln_w.astype(f32) + ln_b.astype(f32)
    bonus = jnp.sum(r * k * r_k.astype(f32), axis=-1, keepdims=True) * v
    y = (y + bonus.reshape(B, S, RWKV_WIDTH)) * g
    return y @ w_o.astype(f32)


def _complex_linear_combine(e1, e2):
    a1r, a1i, b1r, b1i = e1
    a2r, a2i, b2r, b2i = e2
    ar = a2r * a1r - a2i * a1i
    ai = a2r * a1i + a2i * a1r
    br = a2r * b1r - a2i * b1i + b2r
    bi = a2r * b1i + a2i * b1r + b2i
    return ar, ai, br, bi


def _s5_ssm_glu(u, log_dt, a_re, a_im, b_re, b_im, c_re, c_im, d, w_glu):
    f32 = jnp.float32
    B, S, _ = u.shape
    uf = u.astype(f32).reshape(B, S, S5_GROUPS, S5_GROUP)
    a_re = a_re.astype(f32)
    a_im = a_im.astype(f32)
    dt = jnp.exp(log_dt.astype(f32))[:, None]
    mag = jnp.exp(a_re * dt)
    lam_re = mag * jnp.cos(a_im * dt)
    lam_im = mag * jnp.sin(a_im * dt)
    den = a_re * a_re + a_im * a_im
    num_re = lam_re - 1.0
    f_re = (num_re * a_re + lam_im * a_im) / den
    f_im = (lam_im * a_re - num_re * a_im) / den
    b_re = b_re.astype(f32)
    b_im = b_im.astype(f32)
    bb_re = f_re[..., None] * b_re - f_im[..., None] * b_im
    bb_im = f_re[..., None] * b_im + f_im[..., None] * b_re
    bu_re = jnp.einsum('bsgh,gph->bsgp', uf, bb_re)
    bu_im = jnp.einsum('bsgh,gph->bsgp', uf, bb_im)
    lam_re_t = jnp.broadcast_to(lam_re, (1, S, S5_GROUPS, S5_STATE))
    lam_im_t = jnp.broadcast_to(lam_im, (1, S, S5_GROUPS, S5_STATE))
    _, _, xs_re, xs_im = lax.associative_scan(
        _complex_linear_combine, (lam_re_t, lam_im_t, bu_re, bu_im), axis=1)
    y = (jnp.einsum('bsgp,ghp->bsgh', xs_re, c_re.astype(f32))
         - jnp.einsum('bsgp,ghp->bsgh', xs_im, c_im.astype(f32))
         + d.astype(f32) * uf)
    y = jax.nn.gelu(y.reshape(B, S, S5_WIDTH))
    z = y @ w_glu.astype(f32)
    z_val, z_gate = jnp.split(z, 2, axis=-1)
    return z_val * jax.nn.sigmoid(z_gate)


def _peer_ffn(xn, wq, subkeys, table_u, table_v):
    B, S, D = xn.shape
    xb = xn.reshape((B * S) // PEER_BLOCK, PEER_BLOCK, D)

    def block(xt):
        q = (xt @ wq).reshape(PEER_BLOCK, PEER_HEADS, 2, PEER_HALF).astype(jnp.float32)
        s = jnp.einsum('thcd,hcnd->thcn', q, subkeys.astype(jnp.float32))
        sv, si = lax.top_k(s, PEER_TOPK)
        cand = sv[:, :, 0, :, None] + sv[:, :, 1, None, :]
        cv, ci = lax.top_k(cand.reshape(PEER_BLOCK, PEER_HEADS, PEER_TOPK * PEER_TOPK), PEER_TOPK)
        i1 = jnp.take_along_axis(si[:, :, 0], ci // PEER_TOPK, axis=-1)
        i2 = jnp.take_along_axis(si[:, :, 1], ci % PEER_TOPK, axis=-1)
        e = i1 * PEER_KEYS + i2
        gate = jax.nn.softmax(cv, axis=-1)
        u_sel = table_u[e]
        hid = jax.nn.gelu(jnp.einsum('td,thkd->thk', xt, u_sel).astype(jnp.float32))
        return jnp.einsum('thk,thkd->td', (gate * hid).astype(xt.dtype), table_v[e])

    return lax.map(block, xb).reshape(B, S, D)


def setup_inputs(seed: int = 0) -> dict:
    key = jax.random.key(seed)
    ks = jax.random.split(key, 32)
    f32 = jnp.float32
    L = DEPTH

    def nrm(k, shape, scale):
        return jax.random.normal(k, shape, f32) * scale

    def gain(k, shape):
        return 1.0 + 0.02 * jax.random.normal(k, shape, f32)

    a_im_init = math.pi * jnp.arange(S5_STATE, dtype=f32)
    return {
        'x': jax.random.normal(ks[0], (BATCH, SEQ, D_MODEL), f32),
        'norm_mix': gain(ks[1], (L, D_MODEL)),
        'w_in': nrm(ks[2], (L, D_MODEL, N_IN_COLS), D_MODEL ** -0.5),
        'b_gate': nrm(ks[3], (L, 2 * D_MODEL), 0.1),
        'mu_rwkv': jax.random.uniform(ks[4], (L, N_RWKV_COLS), f32),
        'w_lora_up': nrm(ks[5], (L, LORA_W, RWKV_WIDTH), 0.1 * LORA_W ** -0.5),
        'w0': jax.random.uniform(ks[6], (L, RWKV_WIDTH), f32, -6.0, 1.0),
        'a_lora_up': nrm(ks[7], (L, LORA_A, RWKV_WIDTH), 0.1 * LORA_A ** -0.5),
        'a0': nrm(ks[8], (L, RWKV_WIDTH), 0.1),
        'g_lora_up': nrm(ks[9], (L, LORA_G, RWKV_WIDTH), LORA_G ** -0.5),
        'k_k': 0.85 + 0.02 * jax.random.normal(ks[10], (L, RWKV_WIDTH), f32),
        'k_a': gain(ks[11], (L, RWKV_WIDTH)),
        'r_k': nrm(ks[12], (L, RWKV_HEADS, RWKV_HEAD_DIM), 0.1),
        'ln_x_w': gain(ks[13], (L, RWKV_WIDTH)),
        'ln_x_b': nrm(ks[14], (L, RWKV_WIDTH), 0.02),
        'w_o_rwkv': nrm(ks[15], (L, RWKV_WIDTH, D_MODEL), RWKV_WIDTH ** -0.5),
        's5_log_dt': jax.random.uniform(ks[16], (L, S5_GROUPS), f32, math.log(DT_MIN), math.log(DT_MAX)),
        's5_a_re': -0.5 * jnp.exp(0.01 * jax.random.normal(ks[17], (L, S5_GROUPS, S5_STATE), f32)),
        's5_a_im': a_im_init + 0.01 * jax.random.normal(ks[18], (L, S5_GROUPS, S5_STATE), f32),
        's5_b_re': nrm(ks[19], (L, S5_GROUPS, S5_STATE, S5_GROUP), (2 * S5_GROUP) ** -0.5),
        's5_b_im': nrm(ks[20], (L, S5_GROUPS, S5_STATE, S5_GROUP), (2 * S5_GROUP) ** -0.5),
        's5_c_re': nrm(ks[21], (L, S5_GROUPS, S5_GROUP, S5_STATE), (2 * S5_STATE) ** -0.5),
        's5_c_im': nrm(ks[22], (L, S5_GROUPS, S5_GROUP, S5_STATE), (2 * S5_STATE) ** -0.5),
        's5_d': nrm(ks[23], (L, S5_GROUPS, S5_GROUP), 1.0),
        'w_glu_s5': nrm(ks[24], (L, S5_WIDTH, 2 * D_MODEL), S5_WIDTH ** -0.5),
        'w_out': nrm(ks[25], (L, D_MODEL, D_MODEL), D_MODEL ** -0.5),
        'norm_ffn': gain(ks[26], (L, D_MODEL)),
        'peer_wq': nrm(ks[27], (L, D_MODEL, PEER_HEADS * PEER_KEY_DIM), D_MODEL ** -0.5),
        'peer_subkeys': nrm(ks[28], (L, PEER_HEADS, 2, PEER_KEYS, PEER_HALF), PEER_HALF ** -0.5),
        'peer_u': nrm(ks[29], (L, PEER_EXPERTS, D_MODEL), D_MODEL ** -0.5),
        'peer_v': nrm(ks[30], (L, PEER_EXPERTS, D_MODEL), PEER_HEADS ** -0.5),
        'norm_final': gain(ks[31], (D_MODEL,)),
    }


def reference(x, norm_mix, w_in, b_gate, mu_rwkv, w_lora_up, w0, a_lora_up, a0, g_lora_up,
              k_k, k_a, r_k, ln_x_w, ln_x_b, w_o_rwkv, s5_log_dt, s5_a_re, s5_a_im,
              s5_b_re, s5_b_im, s5_c_re, s5_c_im, s5_d, w_glu_s5, w_out, norm_ffn,
              peer_wq, peer_subkeys, peer_u, peer_v, norm_final):
    h = x
    for l in range(DEPTH):
        xn = _rms_norm(h, norm_mix[l])
        p = xn @ w_in[l]
        p_rwkv = p[..., :N_RWKV_COLS]
        p_s5 = p[..., N_RWKV_COLS:N_RWKV_COLS + S5_WIDTH]
        p_gate = p[..., N_RWKV_COLS + S5_WIDTH:].astype(jnp.float32) + b_gate[l].astype(jnp.float32)
        y_a = _rwkv7_time_mix(p_rwkv, mu_rwkv[l], w_lora_up[l], w0[l], a_lora_up[l], a0[l],
                              g_lora_up[l], k_k[l], k_a[l], r_k[l], ln_x_w[l], ln_x_b[l], w_o_rwkv[l])
        y_b = _s5_ssm_glu(p_s5, s5_log_dt[l], s5_a_re[l], s5_a_im[l], s5_b_re[l], s5_b_im[l],
                          s5_c_re[l], s5_c_im[l], s5_d[l], w_glu_s5[l])
        g_a, g_b = jnp.split(jax.nn.sigmoid(p_gate), 2, axis=-1)
        mixed = (g_a * y_a + g_b * y_b).astype(h.dtype)
        h = h + mixed @ w_out[l]
        h = h + _peer_ffn(_rms_norm(h, norm_ffn[l]), peer_wq[l], peer_subkeys[l], peer_u[l], peer_v[l])
    return _rms_norm(h, norm_final)
```

```python
import functools
import math

import jax
import jax.numpy as jnp
from jax import lax
from jax.experimental import pallas as pl
from jax.experimental.pallas import tpu as pltpu

F32 = jnp.float32
BF16 = jnp.bfloat16
I32 = jnp.int32
U32 = jnp.uint32
HI = lax.Precision.HIGHEST

EPS = 1e-6
GN_EPS = 64e-5
D_MODEL = 1024
RW = 512
NH = 8
HD = 64
N_RWKV_COLS = 1792
S5W = 512
S5_STATES = 2048
PEER_HEADS = 8
PEER_KEYS = 128
PEER_TOPK = 16
PEER_HALF = 64
CHUNK = 64

VMEM_LIMIT = 56 * 1024 * 1024


def _dot(a, b, prec=None):
    return jnp.dot(a, b, preferred_element_type=F32, precision=prec)


def _dot_nt(a, b, prec=HI):
    return lax.dot_general(a, b, (((1,), (1,)), ((), ())), preferred_element_type=F32, precision=prec)


def _dot_tn(a, b, prec=HI):
    return lax.dot_general(a, b, (((0,), (0,)), ((), ())), preferred_element_type=F32, precision=prec)


def _sigmoid(x):
    return 1.0 / (1.0 + jnp.exp(-x))


def _softplus(x):
    return jnp.maximum(x, 0.0) + jnp.log(1.0 + jnp.exp(-jnp.abs(x)))


def _gelu(x):
    return 0.5 * x * (1.0 + jnp.tanh(math.sqrt(2.0 / math.pi) * (x + 0.044715 * (x * x * x))))


def _full(shape):
    n = len(shape)
    return pl.BlockSpec(shape, lambda *_: (0,) * n)


def _inproj_body(x_ref, gain_ref, w_ref, bg_ref, mu_ref, pr_ref, ps_ref, g_ref, prev_ref):
    @pl.when(pl.program_id(1) == 0)
    def _():
        prev_ref[...] = jnp.zeros_like(prev_ref)

    x = x_ref[...]
    xn = x * lax.rsqrt(jnp.mean(x * x, axis=-1, keepdims=True) + EPS) * gain_ref[...]
    xb = xn.astype(BF16)
    p = _dot(xb, w_ref[:, :N_RWKV_COLS])
    tt = p.shape[0]
    row = lax.broadcasted_iota(I32, p.shape, 0)
    prev = jnp.broadcast_to(prev_ref[0:1, :], p.shape)
    shifted = jnp.where(row == 0, prev, pltpu.roll(p, 1, 0))
    prev_ref[0:1, :] = p[tt - 1:tt, :]
    pr_ref[...] = p + (shifted - p) * mu_ref[...]
    ps_ref[...] = _dot(xb, w_ref[:, N_RWKV_COLS:N_RWKV_COLS + S5W])
    g_ref[...] = _sigmoid(_dot(xb, w_ref[:, N_RWKV_COLS + S5W:]) + bg_ref[...])


def _inproj(x2, gain, w_in_bf, b_gate, mu, nb, tiles, tt):
    T = x2.shape[0]
    ncols = w_in_bf.shape[1]
    row = lambda b, t: (b * tiles + t, 0)
    return pl.pallas_call(
        _inproj_body,
        grid=(nb, tiles),
        in_specs=[pl.BlockSpec((tt, D_MODEL), row), _full((1, D_MODEL)), _full((D_MODEL, ncols)),
                  _full((1, 2 * D_MODEL)), _full((1, N_RWKV_COLS))],
        out_specs=[pl.BlockSpec((tt, N_RWKV_COLS), row), pl.BlockSpec((tt, S5W), row),
                   pl.BlockSpec((tt, 2 * D_MODEL), row)],
        out_shape=[jax.ShapeDtypeStruct((T, N_RWKV_COLS), F32), jax.ShapeDtypeStruct((T, S5W), F32),
                   jax.ShapeDtypeStruct((T, 2 * D_MODEL), F32)],
        scratch_shapes=[pltpu.VMEM((8, N_RWKV_COLS), F32)],
        compiler_params=pltpu.CompilerParams(dimension_semantics=("arbitrary", "arbitrary"),
                                             vmem_limit_bytes=VMEM_LIMIT),
        name="inproj",
    )(x2, gain, w_in_bf, b_gate, mu)


def _rwkv_body(pr_ref, wup_ref, w0_ref, aup_ref, a0_ref, gup_ref, kk_ref, ka_ref, rk_ref, lnw_ref, lnb_ref,
               bd_ref, wo_ref, ya_ref, st_ref, y_scr, *, nc):
    @pl.when(pl.program_id(1) == 0)
    def _():
        st_ref[...] = jnp.zeros_like(st_ref)

    L = CHUNK
    r = pr_ref[:, 0:RW]
    k = pr_ref[:, RW:2 * RW]
    v = pr_ref[:, 2 * RW:3 * RW]
    x128 = pr_ref[:, 3 * RW:3 * RW + 128]
    xg = pr_ref[:, 3 * RW + 128:3 * RW + 256]
    bd = bd_ref[...]

    w_log = -_softplus(-(w0_ref[...] + _dot(jnp.tanh(x128), wup_ref[...], HI))) - 0.5
    logw = -jnp.exp(w_log)
    a_lr = _sigmoid(a0_ref[...] + _dot(x128, aup_ref[...], HI))
    g = _dot(_sigmoid(xg), gup_ref[...], HI)
    kk = k * kk_ref[...]
    k2 = k * (1.0 + (a_lr - 1.0) * ka_ref[...])
    kk = kk / jnp.maximum(jnp.sqrt(_dot(kk * kk, bd, HI)), 1e-12)
    b = kk * a_lr
    am = -kk

    ri = lax.broadcasted_iota(I32, (L, L), 0)
    ci = lax.broadcasted_iota(I32, (L, L), 1)
    strict = ri > ci
    incl = ri >= ci
    eye = (ri == ci).astype(F32)
    tri = incl.astype(F32)

    for c_i in range(nc):
        rows = slice(c_i * L, (c_i + 1) * L)
        lw = logw[rows]
        c = _dot(tri, lw, HI)
        cl = c[L - 1:L, :]
        e_c = jnp.exp(c)
        e_nc = jnp.exp(-c)
        e_cp = jnp.exp(c - lw)
        e_rem = jnp.exp(cl - c)
        e_last = jnp.exp(cl)
        at_all = am[rows] * e_cp
        rt_all = r[rows] * e_c
        bt_all = b[rows] * e_nc
        kt_all = k2[rows] * e_nc
        bh_all = b[rows] * e_rem
        kh_all = k2[rows] * e_rem
        v_all = v[rows]
        for h in range(NH):
            hs = slice(h * HD, (h + 1) * HD)
            at, rt, bt, kt = at_all[:, hs], rt_all[:, hs], bt_all[:, hs], kt_all[:, hs]
            bh, kh, vh = bh_all[:, hs], kh_all[:, hs], v_all[:, hs]
            q = _dot_nt(jnp.concatenate([at, rt], axis=0), jnp.concatenate([bt, kt], axis=0))
            a_ab = jnp.where(strict, q[:L, :L], 0.0)
            a_ak = jnp.where(strict, q[:L, L:], 0.0)
            a_rb = jnp.where(incl, q[L:, :L], 0.0)
            a_rk = jnp.where(incl, q[L:, L:], 0.0)
            pw = a_ab
            tinv = eye + a_ab
            step = 1
            while 2 * step < L:
                pw = _dot(pw, pw, HI)
                tinv = tinv + _dot(pw, tinv, HI)
                step *= 2
            pw = _dot(pw, pw, HI)
            tinv = tinv + _dot(pw, tinv, HI)
            av = _dot(jnp.concatenate([a_ak, a_rk], axis=0), vh, HI)
            at2 = _dot(tinv, at, HI)
            u0 = _dot(tinv, av[:L], HI)
            m = eye * e_last[:, hs] + _dot_tn(at2, bh)
            cc = _dot_tn(u0, bh) + _dot_tn(vh, kh)
            r2 = rt + _dot(a_rb, at2, HI)
            y0 = _dot(a_rb, u0, HI) + av[L:]
            s0 = st_ref[h]
            y_scr[rows, hs] = y0 + _dot_nt(r2, s0)
            st_ref[h] = _dot(s0, m, HI) + cc

    y = y_scr[...]
    mean = _dot(y, bd, HI) * (1.0 / HD)
    yc = y - mean
    var = _dot(yc * yc, bd, HI) * (1.0 / HD)
    yn = yc * lax.rsqrt(var + GN_EPS) * lnw_ref[...] + lnb_ref[...]
    bonus = _dot(r * k2 * rk_ref[...], bd, HI) * v
    out = (yn + bonus) * g
    ya_ref[...] = _dot(out.astype(BF16), wo_ref[...])


def _rwkv(pr, wup, w0, aup, a0, gup, k_k, k_a, r_k, ln_w, ln_b, bd, wo_bf, nb, tiles, tc):
    T = pr.shape[0]
    row = lambda b, t: (b * tiles + t, 0)
    return pl.pallas_call(
        functools.partial(_rwkv_body, nc=tc // CHUNK),
        grid=(nb, tiles),
        in_specs=[pl.BlockSpec((tc, N_RWKV_COLS), row), _full((128, RW)), _full((1, RW)), _full((128, RW)),
                  _full((1, RW)), _full((128, RW)), _full((1, RW)), _full((1, RW)), _full((1, RW)),
                  _full((1, RW)), _full((1, RW)), _full((RW, RW)), _full((RW, D_MODEL))],
        out_specs=pl.BlockSpec((tc, D_MODEL), row),
        out_shape=jax.ShapeDtypeStruct((T, D_MODEL), F32),
        scratch_shapes=[pltpu.VMEM((NH, HD, HD), F32), pltpu.VMEM((tc, RW), F32)],
        compiler_params=pltpu.CompilerParams(dimension_semantics=("arbitrary", "arbitrary"),
                                             vmem_limit_bytes=VMEM_LIMIT),
        name="rwkv",
    )(pr, wup, w0, aup, a0, gup, k_k, k_a, r_k, ln_w, ln_b, bd, wo_bf)


def _s5_prep_body(ldt_r, are_r, aim_r, ldt_c, are_c, aim_c, bre_ref, bim_ref, lpr_ref, lpi_ref, bbr_ref, bbi_ref, *, tc):
    dt = jnp.exp(ldt_r[...])
    mag = jnp.exp(are_r[...] * dt)
    ang = aim_r[...] * dt
    lpr_ref[0:1, :] = mag * jnp.cos(ang)
    lpi_ref[0:1, :] = mag * jnp.sin(ang)
    n = 1
    while n < tc:
        pr = lpr_ref[n - 1:n, :]
        pi = lpi_ref[n - 1:n, :]
        qr = lpr_ref[0:n, :]
        qi = lpi_ref[0:n, :]
        lpr_ref[n:2 * n, :] = qr * pr - qi * pi
        lpi_ref[n:2 * n, :] = qr * pi + qi * pr
        n *= 2
    dtc = jnp.exp(ldt_c[...])
    are = are_c[...]
    aim = aim_c[...]
    magc = jnp.exp(are * dtc)
    angc = aim * dtc
    lre = magc * jnp.cos(angc)
    lim = magc * jnp.sin(angc)
    den = are * are + aim * aim
    nre = lre - 1.0
    fre = (nre * are + lim * aim) / den
    fim = (lim * are - nre * aim) / den
    bre = bre_ref[...]
    bim = bim_ref[...]
    bbr_ref[...] = fre * bre - fim * bim
    bbi_ref[...] = fre * bim + fim * bre


def _s5_prep(log_dt, a_re, a_im, b_re, b_im, tc):
    ng, ns = a_re.shape
    hh = b_re.shape[-1]
    n = ng * ns
    ldt = jnp.broadcast_to(log_dt[:, None], (ng, ns))
    row = lambda z: z.reshape(1, n)
    col = lambda z: z.reshape(n, 1)
    return pl.pallas_call(
        functools.partial(_s5_prep_body, tc=tc),
        out_shape=[jax.ShapeDtypeStruct((tc, n), F32), jax.ShapeDtypeStruct((tc, n), F32),
                   jax.ShapeDtypeStruct((n, hh), F32), jax.ShapeDtypeStruct((n, hh), F32)],
        name="s5_prep",
    )(row(ldt), row(a_re), row(a_im), col(ldt), col(a_re), col(a_im), b_re.reshape(n, hh), b_im.reshape(n, hh))


def _s5_body(u_ref, bw_ref, cw_ref, d_ref, lpr_ref, lpi_ref, wglu_ref, yb_ref, st_ref):
    @pl.when(pl.program_id(1) == 0)
    def _():
        st_ref[...] = jnp.zeros_like(st_ref)

    u = u_ref[...]
    tc = u.shape[0]
    nblk = bw_ref.shape[0]
    sb = S5_STATES // nblk
    ub = S5W // nblk
    bus = [_dot(u[:, j * ub:(j + 1) * ub], bw_ref[j], HI) for j in range(nblk)]
    xr = jnp.concatenate([bu[:, :sb] for bu in bus], axis=1)
    xi = jnp.concatenate([bu[:, sb:] for bu in bus], axis=1)
    row = lax.broadcasted_iota(I32, (tc, 1), 0)
    d = 1
    while d < tc:
        lr = lpr_ref[d - 1:d, :]
        li = lpi_ref[d - 1:d, :]
        keep = row >= d
        sr = jnp.where(keep, pltpu.roll(xr, d, 0), 0.0)
        si = jnp.where(keep, pltpu.roll(xi, d, 0), 0.0)
        xr, xi = xr + lr * sr - li * si, xi + lr * si + li * sr
        d *= 2
    pr = st_ref[0:1, :]
    pi = st_ref[1:2, :]
    lr = lpr_ref[...]
    li = lpi_ref[...]
    xr, xi = xr + lr * pr - li * pi, xi + lr * pi + li * pr
    st_ref[0:1, :] = xr[tc - 1:tc, :]
    st_ref[1:2, :] = xi[tc - 1:tc, :]
    ys = [_dot(jnp.concatenate([xr[:, j * sb:(j + 1) * sb], xi[:, j * sb:(j + 1) * sb]], axis=1), cw_ref[j], HI)
          for j in range(nblk)]
    y = jnp.concatenate(ys, axis=1) + d_ref[...] * u
    z = _dot(_gelu(y).astype(BF16), wglu_ref[...])
    yb_ref[...] = z[:, :D_MODEL] * _sigmoid(z[:, D_MODEL:])


def _s5(ps, bw, cw, dd, lpr, lpi, wglu_bf, nb, tiles, tc):
    T = ps.shape[0]
    row = lambda b, t: (b * tiles + t, 0)
    return pl.pallas_call(
        _s5_body,
        grid=(nb, tiles),
        in_specs=[pl.BlockSpec((tc, S5W), row), _full(bw.shape), _full(cw.shape), _full((1, S5W)),
                  _full((tc, S5_STATES)), _full((tc, S5_STATES)), _full((S5W, 2 * D_MODEL))],
        out_specs=pl.BlockSpec((tc, D_MODEL), row),
        out_shape=jax.ShapeDtypeStruct((T, D_MODEL), F32),
        scratch_shapes=[pltpu.VMEM((8, S5_STATES), F32)],
        compiler_params=pltpu.CompilerParams(dimension_semantics=("arbitrary", "arbitrary"),
                                             vmem_limit_bytes=VMEM_LIMIT),
        name="s5",
    )(ps, bw, cw, dd, lpr, lpi, wglu_bf)


def _stage2_layout():
    blocks = [("row", 0, 16), ("row", 1, 8), ("row", 2, 8), ("row", 3, 8),
              ("col", 0, 16), ("col", 1, 8), ("col", 2, 8)]
    flat = []
    for kind, a, n in blocks:
        for m in range(n):
            i, j = (a, m) if kind == "row" else (m, a)
            ok = (i + 1) * (j + 1) <= PEER_TOPK and (kind == "row" or i >= 4)
            flat.append(i * PEER_TOPK + j if ok else -1)
    return blocks, flat


def _topk_rows(vals, key, n_out, big):
    out_v, out_k = [], []
    for _ in range(n_out):
        m = jnp.max(vals, axis=0, keepdims=True)
        sel = jnp.min(jnp.where(vals == m, key, big), axis=0, keepdims=True)
        out_v.append(m)
        out_k.append(sel)
        vals = jnp.where(key == sel, -jnp.inf, vals)
    return out_v, out_k


def _mix_body(x_ref, ya_ref, yb_ref, g_ref, wout_ref, nf_ref, wq_ref, keys_ref, flat_ref,
              h_ref, xn_ref, ids_ref, gate_ref, sv_scr, si_scr):
    g = g_ref[...]
    mixed = g[:, :D_MODEL] * ya_ref[...] + g[:, D_MODEL:] * yb_ref[...]
    h = x_ref[...] + _dot(mixed.astype(BF16), wout_ref[...])
    h_ref[...] = h
    xn = h * lax.rsqrt(jnp.mean(h * h, axis=-1, keepdims=True) + EPS) * nf_ref[...]
    xn_ref[...] = xn
    xh = xn.astype(BF16)
    xl = (xn - xh.astype(F32)).astype(BF16)
    q = _dot(xh, wq_ref[0]) + (_dot(xl, wq_ref[0]) + _dot(xh, wq_ref[1]))
    tt = q.shape[0]
    key_iota = lax.broadcasted_iota(I32, (PEER_KEYS, tt), 0)
    blocks, _ = _stage2_layout()
    flat = flat_ref[...]
    flat_b = jnp.broadcast_to(flat, (flat.shape[0], tt))
    for hd in range(PEER_HEADS):
        for half in range(2):
            col = (hd * 2 + half) * PEER_HALF
            s_t = _dot_nt(keys_ref[hd * 2 + half], q[:, col:col + PEER_HALF])
            vs, ks = _topk_rows(s_t, key_iota, PEER_TOPK, PEER_KEYS)
            for i in range(PEER_TOPK):
                sv_scr[half, i:i + 1, :] = vs[i]
                si_scr[half, i:i + 1, :] = ks[i]
        sv1, sv2 = sv_scr[0], sv_scr[1]
        si1, si2 = si_scr[0], si_scr[1]
        cand, eid = [], []
        for kind, a, n in blocks:
            if kind == "row":
                cand.append(sv1[a:a + 1, :] + sv2[0:n, :])
                eid.append(si1[a:a + 1, :] * PEER_KEYS + si2[0:n, :])
            else:
                cand.append(sv1[0:n, :] + sv2[a:a + 1, :])
                eid.append(si1[0:n, :] * PEER_KEYS + si2[a:a + 1, :])
        cand = jnp.where(flat_b >= 0, jnp.concatenate(cand, axis=0), -jnp.inf)
        eid = jnp.concatenate(eid, axis=0)
        cv, cf = _topk_rows(cand, flat_b, PEER_TOPK, PEER_TOPK * PEER_TOPK)
        ex = [jnp.exp(c - cv[0]) for c in cv]
        tot = ex[0]
        for e in ex[1:]:
            tot = tot + e
        for i in range(PEER_TOPK):
            r = hd * PEER_TOPK + i
            ids_ref[r:r + 1, :] = jnp.sum(jnp.where(flat_b == cf[i], eid, 0), axis=0, keepdims=True)
            gate_ref[r:r + 1, :] = ex[i] / tot


def _mix(x2, ya, yb, g, wout_bf, norm_ffn, wq_bf, keys, flat, tt):
    T = x2.shape[0]
    row = lambda i: (i, 0)
    colb = lambda i: (0, i)
    nk = PEER_HEADS * PEER_TOPK
    return pl.pallas_call(
        _mix_body,
        grid=(T // tt,),
        in_specs=[pl.BlockSpec((tt, D_MODEL), row), pl.BlockSpec((tt, D_MODEL), row),
                  pl.BlockSpec((tt, D_MODEL), row), pl.BlockSpec((tt, 2 * D_MODEL), row),
                  _full((D_MODEL, D_MODEL)), _full((1, D_MODEL)), _full((2, D_MODEL, D_MODEL)),
                  _full(keys.shape), _full(flat.shape)],
        out_specs=[pl.BlockSpec((tt, D_MODEL), row), pl.BlockSpec((tt, D_MODEL), row),
                   pl.BlockSpec((nk, tt), colb), pl.BlockSpec((nk, tt), colb)],
        out_shape=[jax.ShapeDtypeStruct((T, D_MODEL), F32), jax.ShapeDtypeStruct((T, D_MODEL), F32),
                   jax.ShapeDtypeStruct((nk, T), I32), jax.ShapeDtypeStruct((nk, T), F32)],
        scratch_shapes=[pltpu.VMEM((2, PEER_TOPK, tt), F32), pltpu.VMEM((2, PEER_TOPK, tt), I32)],
        compiler_params=pltpu.CompilerParams(dimension_semantics=("arbitrary",),
                                             vmem_limit_bytes=VMEM_LIMIT),
        name="mix_topk",
    )(x2, ya, yb, g, wout_bf, norm_ffn, wq_bf, keys, flat)


ROW_WORDS = 4
GROUP = 8
HI_MASK = 0xFFFF0000


def _pack_table(tab):
    bits = lax.bitcast_convert_type(tab.astype(BF16), jnp.uint16).astype(U32)
    half = tab.shape[1] // 2
    packed = bits[:, :half] | (bits[:, half:] << 16)
    return packed.reshape(tab.shape[0] * ROW_WORDS, 128)


def _unpack(words):
    lo = pltpu.bitcast(words << 16, F32)
    hi = pltpu.bitcast(words & jnp.uint32(HI_MASK), F32)
    return lo, hi


def _peer_hidden_body(ids_ref, x_ref, gate_ref, tab_ref, c_ref, g_scr, hid_scr):
    nk = PEER_HEADS * PEER_TOPK
    tb = x_ref.shape[0]

    def group(gi, carry):
        base = pl.multiple_of(gi * GROUP, GROUP)
        for i in range(GROUP):
            t = base + i
            for mi in range(nk):
                e = ids_ref[t, mi]
                g_scr[mi * ROW_WORDS:(mi + 1) * ROW_WORDS, :] = tab_ref[pl.ds(pl.multiple_of(e * ROW_WORDS, ROW_WORDS), ROW_WORDS), :]
            xrow = x_ref[pl.ds(t, 1), :]
            acc = jnp.zeros((nk, 128), F32)
            for j in range(ROW_WORDS):
                lo, hi = _unpack(g_scr[pl.ds(j, nk, stride=ROW_WORDS), :])
                xa = xrow[:, j * 128:(j + 1) * 128]
                xb = xrow[:, 512 + j * 128:512 + (j + 1) * 128]
                acc = acc + lo * xa + hi * xb
            hid_scr[i:i + 1, :] = jnp.sum(acc.T, axis=0, keepdims=True)
        hid = hid_scr[...]
        c_ref[pl.ds(base, GROUP), :] = gate_ref[pl.ds(base, GROUP), :] * _gelu(hid)
        return carry

    lax.fori_loop(0, tb // GROUP, group, 0)


def _peer_hidden(ids, xn, gate, tab, tb):
    T = xn.shape[0]
    nk = PEER_HEADS * PEER_TOPK
    row = lambda i: (i, 0)
    return pl.pallas_call(
        _peer_hidden_body,
        grid=(T // tb,),
        in_specs=[pl.BlockSpec((tb, nk), row, memory_space=pltpu.SMEM), pl.BlockSpec((tb, D_MODEL), row),
                  pl.BlockSpec((tb, nk), row), _full(tab.shape)],
        out_specs=pl.BlockSpec((tb, nk), row),
        out_shape=jax.ShapeDtypeStruct((T, nk), F32),
        scratch_shapes=[pltpu.VMEM((nk * ROW_WORDS, 128), U32), pltpu.VMEM((GROUP, 128), F32)],
        compiler_params=pltpu.CompilerParams(dimension_semantics=("arbitrary",),
                                             vmem_limit_bytes=VMEM_LIMIT),
        name="peer_hidden",
    )(ids, xn, gate, tab)


def _peer_out_body(ids_ref, c_ref, tab_ref, o_ref):
    nk = PEER_HEADS * PEER_TOPK
    tb = o_ref.shape[0]

    def token(t, carry):
        acc_lo = jnp.zeros((ROW_WORDS, 128), F32)
        acc_hi = jnp.zeros((ROW_WORDS, 128), F32)
        for mi in range(nk):
            e = ids_ref[t, mi]
            w = c_ref[t, mi]
            lo, hi = _unpack(tab_ref[pl.ds(pl.multiple_of(e * ROW_WORDS, ROW_WORDS), ROW_WORDS), :])
            acc_lo = acc_lo + w * lo
            acc_hi = acc_hi + w * hi
        o_ref[t] = jnp.concatenate([acc_lo, acc_hi], axis=0)
        return carry

    lax.fori_loop(0, tb, token, 0)


def _peer_out(ids, c, tab, tb):
    T = ids.shape[0]
    nk = PEER_HEADS * PEER_TOPK
    row = lambda i: (i, 0)
    return pl.pallas_call(
        _peer_out_body,
        grid=(T // tb,),
        in_specs=[pl.BlockSpec((tb, nk), row, memory_space=pltpu.SMEM),
                  pl.BlockSpec((tb, nk), row, memory_space=pltpu.SMEM), _full(tab.shape)],
        out_specs=pl.BlockSpec((tb, 2 * ROW_WORDS, 128), lambda i: (i, 0, 0)),
        out_shape=jax.ShapeDtypeStruct((T, 2 * ROW_WORDS, 128), F32),
        compiler_params=pltpu.CompilerParams(dimension_semantics=("arbitrary",),
                                             vmem_limit_bytes=VMEM_LIMIT),
        name="peer_out",
    )(ids, c, tab)


def _final_body(h_ref, p_ref, gain_ref, o_ref):
    x = h_ref[...] + p_ref[...]
    o_ref[...] = x * lax.rsqrt(jnp.mean(x * x, axis=-1, keepdims=True) + EPS) * gain_ref[...]


def _final(h, p, gain, tt):
    T = h.shape[0]
    row = lambda i: (i, 0)
    return pl.pallas_call(
        _final_body,
        grid=(T // tt,),
        in_specs=[pl.BlockSpec((tt, D_MODEL), row), pl.BlockSpec((tt, D_MODEL), row), _full((1, D_MODEL))],
        out_specs=pl.BlockSpec((tt, D_MODEL), row),
        out_shape=jax.ShapeDtypeStruct((T, D_MODEL), F32),
        compiler_params=pltpu.CompilerParams(dimension_semantics=("arbitrary",)),
        name="final_norm",
    )(h, p, gain)


def _tile(n, pref):
    t = min(n, pref)
    assert n % t == 0, (n, t)
    return t


def kernel(x, norm_mix, w_in, b_gate, mu_rwkv, w_lora_up, w0, a_lora_up, a0, g_lora_up, k_k, k_a, r_k, ln_x_w, ln_x_b, w_o_rwkv, s5_log_dt, s5_a_re, s5_a_im, s5_b_re, s5_b_im, s5_c_re, s5_c_im, s5_d, w_glu_s5, w_out, norm_ffn, peer_wq, peer_subkeys, peer_u, peer_v, norm_final):
    B, S, D = x.shape
    assert D == D_MODEL and w_in.shape[0] == 1 and S % CHUNK == 0
    T = B * S
    x2 = x.reshape(T, D)
    r1 = lambda z: z.reshape(1, -1)

    tt = _tile(S, 256)
    pr, ps, g = _inproj(x2, r1(norm_mix[0]), w_in[0].astype(BF16), r1(b_gate[0]), r1(mu_rwkv[0]), B, S // tt, tt)

    zpad = jnp.zeros((64, RW), F32)
    wup = jnp.concatenate([w_lora_up[0], zpad], axis=0)
    aup = jnp.concatenate([zpad, a_lora_up[0]], axis=0)
    hid = jnp.arange(RW) // HD
    bd = (hid[:, None] == hid[None, :]).astype(F32)
    tc = _tile(S, 256)
    ya = _rwkv(pr, wup, r1(w0[0]), aup, r1(a0[0]), g_lora_up[0], r1(k_k[0]), r1(k_a[0]), r1(r_k[0]),
               r1(ln_x_w[0]), r1(ln_x_b[0]), bd, w_o_rwkv[0].astype(BF16), B, S // tc, tc)

    ts = _tile(S, 256)
    lpr, lpi, bbr, bbi = _s5_prep(s5_log_dt[0], s5_a_re[0], s5_a_im[0], s5_b_re[0], s5_b_im[0], ts)
    ng, ns = s5_a_re[0].shape
    hh = s5_b_re.shape[-1]
    nblk, gb = 4, ng // 4
    eye_g = jnp.eye(gb, dtype=F32)

    def in_blocks(bb):
        return jnp.einsum("jgph,gk->jghkp", bb.reshape(nblk, gb, ns, hh), eye_g).reshape(nblk, gb * hh, gb * ns)

    def out_blocks(cc):
        return jnp.einsum("jghp,gk->jgpkh", cc.reshape(nblk, gb, hh, ns), eye_g).reshape(nblk, gb * ns, gb * hh)

    bw = jnp.concatenate([in_blocks(bbr), in_blocks(bbi)], axis=2)
    cw = jnp.concatenate([out_blocks(s5_c_re[0]), -out_blocks(s5_c_im[0])], axis=1)
    yb = _s5(ps, bw, cw, r1(s5_d[0]), lpr, lpi, w_glu_s5[0].astype(BF16), B, S // ts, ts)

    _, flat = _stage2_layout()
    flat = jnp.asarray(flat, I32).reshape(-1, 1)
    keys = peer_subkeys[0].reshape(PEER_HEADS * 2, PEER_KEYS, PEER_HALF)
    tm = _tile(T, 256)
    wq_hi = peer_wq[0].astype(BF16)
    wq_lo = (peer_wq[0] - wq_hi.astype(F32)).astype(BF16)
    h, xn, ids_t, gate_t = _mix(x2, ya, yb, g, w_out[0].astype(BF16), r1(norm_ffn[0]), jnp.stack([wq_hi, wq_lo]),
                                keys, flat, tm)
    ids = ids_t.T
    gate = gate_t.T

    tb = _tile(T, 256)
    c = _peer_hidden(ids, xn, gate, _pack_table(peer_u[0]), tb)
    po = _peer_out(ids, c, _pack_table(peer_v[0]), tb).reshape(T, D)
    out = _final(h, po, r1(norm_final), _tile(T, 512))
    return out.reshape(B, S, D)
```

```python
import functools
import math

import jax
import jax.numpy as jnp
from jax import lax
from jax.experimental import pallas as pl
from jax.experimental.pallas import tpu as pltpu

F32 = jnp.float32
BF16 = jnp.bfloat16
I32 = jnp.int32
U32 = jnp.uint32
HI = lax.Precision.HIGHEST

EPS = 1e-6
GN_EPS = 64e-5
D_MODEL = 1024
RW = 512
NH = 8
HD = 64
N_RWKV_COLS = 1792
S5W = 512
S5_STATES = 2048
PEER_HEADS = 8
PEER_KEYS = 128
PEER_TOPK = 16
PEER_HALF = 64
CHUNK = 64

VMEM_LIMIT = 56 * 1024 * 1024


def _dot(a, b, prec=None):
    return jnp.dot(a, b, preferred_element_type=F32, precision=prec)


def _dot_nt(a, b, prec=HI):
    return lax.dot_general(a, b, (((1,), (1,)), ((), ())), preferred_element_type=F32, precision=prec)


def _dot_tn(a, b, prec=HI):
    return lax.dot_general(a, b, (((0,), (0,)), ((), ())), preferred_element_type=F32, precision=prec)


_NN = (((1,), (0,)), ((), ()))
_NT = (((1,), (1,)), ((), ()))
_TN = (((0,), (0,)), ((), ()))


def _split(x):
    hi = x.astype(BF16)
    return hi, (x - hi.astype(F32)).astype(BF16)


def _dot3(a, b, dims=_NN):
    dg = lambda p, q: lax.dot_general(p, q, dims, preferred_element_type=F32)
    return dg(a[0], b[0]) + (dg(a[0], b[1]) + dg(a[1], b[0]))


def _dot2(a, b, dims=_NN):
    dg = lambda p, q: lax.dot_general(p, q, dims, preferred_element_type=F32)
    return dg(a[0], b) + dg(a[1], b)


def _sigmoid(x):
    return 1.0 / (1.0 + jnp.exp(-x))


def _softplus(x):
    return jnp.maximum(x, 0.0) + jnp.log(1.0 + jnp.exp(-jnp.abs(x)))


def _gelu(x):
    return 0.5 * x * (1.0 + jnp.tanh(math.sqrt(2.0 / math.pi) * (x + 0.044715 * (x * x * x))))


def _full(shape):
    n = len(shape)
    return pl.BlockSpec(shape, lambda *_: (0,) * n)


def _inproj_body(x_ref, gain_ref, w_ref, bg_ref, mu_ref, pr_ref, ps_ref, g_ref, prev_ref):
    @pl.when(pl.program_id(1) == 0)
    def _():
        prev_ref[...] = jnp.zeros_like(prev_ref)

    x = x_ref[...]
    xn = x * lax.rsqrt(jnp.mean(x * x, axis=-1, keepdims=True) + EPS) * gain_ref[...]
    xb = xn.astype(BF16)
    p = _dot(xb, w_ref[:, :N_RWKV_COLS])
    tt = p.shape[0]
    row = lax.broadcasted_iota(I32, p.shape, 0)
    prev = jnp.broadcast_to(prev_ref[0:1, :], p.shape)
    shifted = jnp.where(row == 0, prev, pltpu.roll(p, 1, 0))
    prev_ref[0:1, :] = p[tt - 1:tt, :]
    pr_ref[...] = p + (shifted - p) * mu_ref[...]
    ps_ref[...] = _dot(xb, w_ref[:, N_RWKV_COLS:N_RWKV_COLS + S5W])
    g_ref[...] = _sigmoid(_dot(xb, w_ref[:, N_RWKV_COLS + S5W:]) + bg_ref[...])


def _inproj(x2, gain, w_in_bf, b_gate, mu, nb, tiles, tt):
    T = x2.shape[0]
    ncols = w_in_bf.shape[1]
    row = lambda b, t: (b * tiles + t, 0)
    return pl.pallas_call(
        _inproj_body,
        grid=(nb, tiles),
        in_specs=[pl.BlockSpec((tt, D_MODEL), row), _full((1, D_MODEL)), _full((D_MODEL, ncols)),
                  _full((1, 2 * D_MODEL)), _full((1, N_RWKV_COLS))],
        out_specs=[pl.BlockSpec((tt, N_RWKV_COLS), row), pl.BlockSpec((tt, S5W), row),
                   pl.BlockSpec((tt, 2 * D_MODEL), row)],
        out_shape=[jax.ShapeDtypeStruct((T, N_RWKV_COLS), F32), jax.ShapeDtypeStruct((T, S5W), F32),
                   jax.ShapeDtypeStruct((T, 2 * D_MODEL), F32)],
        scratch_shapes=[pltpu.VMEM((8, N_RWKV_COLS), F32)],
        compiler_params=pltpu.CompilerParams(dimension_semantics=("arbitrary", "arbitrary"),
                                             vmem_limit_bytes=VMEM_LIMIT),
        name="inproj",
    )(x2, gain, w_in_bf, b_gate, mu)


def _rwkv_body(pr_ref, wup_ref, w0_ref, aup_ref, a0_ref, gup_ref, kk_ref, ka_ref, rk_ref, lnw_ref, lnb_ref,
               bd_ref, wo_ref, ya_ref, st_ref, y_scr, *, nc):
    @pl.when(pl.program_id(1) == 0)
    def _():
        st_ref[...] = jnp.zeros_like(st_ref)

    L = CHUNK
    r = pr_ref[:, 0:RW]
    k = pr_ref[:, RW:2 * RW]
    v = pr_ref[:, 2 * RW:3 * RW]
    x128 = pr_ref[:, 3 * RW:3 * RW + 128]
    xg = pr_ref[:, 3 * RW + 128:3 * RW + 256]
    bd = bd_ref[...]
    seg = lambda z: _dot2(_split(z), bd)

    w_log = -_softplus(-(w0_ref[...] + _dot(jnp.tanh(x128), wup_ref[...], HI))) - 0.5
    logw = -jnp.exp(w_log)
    a_lr = _sigmoid(a0_ref[...] + _dot(x128, aup_ref[...], HI))
    g = _dot(_sigmoid(xg), gup_ref[...], HI)
    kk = k * kk_ref[...]
    k2 = k * (1.0 + (a_lr - 1.0) * ka_ref[...])
    kk = kk / jnp.maximum(jnp.sqrt(seg(kk * kk)), 1e-12)
    b = kk * a_lr
    am = -kk

    ri = lax.broadcasted_iota(I32, (L, L), 0)
    ci = lax.broadcasted_iota(I32, (L, L), 1)
    strict = ri > ci
    incl = ri >= ci
    eye = (ri == ci).astype(F32)
    tri = incl.astype(BF16)
    quad = []
    bit = 0
    while (1 << bit) < L:
        s = 1 << bit
        quad.append((((ri >> (bit + 1)) == (ci >> (bit + 1))) & ((ri & s) != 0) & ((ci & s) == 0)).astype(F32))
        bit += 1
    sl = [slice(h * HD, (h + 1) * HD) for h in range(NH)]
    hr = range(NH)

    m_all, cc_all, r2_all, y0_all = [], [], [], []
    for c_i in range(nc):
        rows = slice(c_i * L, (c_i + 1) * L)
        lw = logw[rows]
        lw_h = lw.astype(BF16)
        lw_r = lw - lw_h.astype(F32)
        lw_m = lw_r.astype(BF16)
        lw_l = (lw_r - lw_m.astype(F32)).astype(BF16)
        c = _dot(tri, lw_h) + (_dot(tri, lw_m) + _dot(tri, lw_l))
        cl = c[L - 1:L, :]
        e_c = jnp.exp(c)
        e_nc = jnp.exp(-c)
        e_cp = jnp.exp(c - lw)
        e_rem = jnp.exp(cl - c)
        e_last = jnp.exp(cl)
        at_all = am[rows] * e_cp
        rt_all = r[rows] * e_c
        bt_all = b[rows] * e_nc
        kt_all = k2[rows] * e_nc
        bh_all = b[rows] * e_rem
        kh_all = k2[rows] * e_rem
        v_all = v[rows]
        at = [at_all[:, s] for s in sl]
        rt = [rt_all[:, s] for s in sl]
        q = [_dot3(_split(jnp.concatenate([at[h], rt[h]], axis=0)),
                   _split(jnp.concatenate([bt_all[:, sl[h]], kt_all[:, sl[h]]], axis=0)), _NT) for h in hr]
        n_ab = [jnp.where(strict, q[h][:L, :L], 0.0) for h in hr]
        a_kr = [_split(jnp.concatenate([jnp.where(strict, q[h][:L, L:], 0.0),
                                        jnp.where(incl, q[h][L:, L:], 0.0)], axis=0)) for h in hr]
        a_rb = [_split(jnp.where(incl, q[h][L:, :L], 0.0)) for h in hr]
        dinv = [eye + n_ab[h] * quad[0] for h in hr]
        for qm in quad[1:]:
            ds = [_split(dinv[h]) for h in hr]
            t1 = [_dot3(_split(n_ab[h] * qm), ds[h]) for h in hr]
            dinv = [dinv[h] + _dot3(ds[h], _split(t1[h])) for h in hr]
        ds = [_split(dinv[h]) for h in hr]
        vs = [_split(v_all[:, s]) for s in sl]
        av = [_dot3(a_kr[h], vs[h]) for h in hr]
        at2 = [_dot3(ds[h], _split(at[h])) for h in hr]
        u0 = [_dot3(ds[h], _split(av[h][:L])) for h in hr]
        at2s = [_split(at2[h]) for h in hr]
        u0s = [_split(u0[h]) for h in hr]
        bhs = [_split(bh_all[:, s]) for s in sl]
        khs = [_split(kh_all[:, s]) for s in sl]
        m_all.append([eye * e_last[:, sl[h]] + _dot3(_split(at2[h].T), bhs[h]) for h in hr])
        cc_all.append([_dot3(_split(u0[h].T), bhs[h]) + _dot3(_split(v_all[:, sl[h]].T), khs[h]) for h in hr])
        r2_all.append([_split(rt[h] + _dot3(a_rb[h], at2s[h])) for h in hr])
        y0_all.append([_dot3(a_rb[h], u0s[h]) + av[h][L:] for h in hr])

    st = [st_ref[h] for h in hr]
    for c_i in range(nc):
        rows = slice(c_i * L, (c_i + 1) * L)
        ss = [_split(st[h]) for h in hr]
        for h in hr:
            y_scr[rows, sl[h]] = y0_all[c_i][h] + _dot3(r2_all[c_i][h], ss[h], _NT)
        st = [_dot3(ss[h], _split(m_all[c_i][h])) + cc_all[c_i][h] for h in hr]
    for h in hr:
        st_ref[h] = st[h]

    y = y_scr[...]
    mean = seg(y) * (1.0 / HD)
    yc = y - mean
    var = seg(yc * yc) * (1.0 / HD)
    yn = yc * lax.rsqrt(var + GN_EPS) * lnw_ref[...] + lnb_ref[...]
    bonus = seg(r * k2 * rk_ref[...]) * v
    out = (yn + bonus) * g
    ya_ref[...] = _dot(out.astype(BF16), wo_ref[...])


def _rwkv(pr, wup, w0, aup, a0, gup, k_k, k_a, r_k, ln_w, ln_b, bd, wo_bf, nb, tiles, tc):
    T = pr.shape[0]
    row = lambda b, t: (b * tiles + t, 0)
    return pl.pallas_call(
        functools.partial(_rwkv_body, nc=tc // CHUNK),
        grid=(nb, tiles),
        in_specs=[pl.BlockSpec((tc, N_RWKV_COLS), row), _full((128, RW)), _full((1, RW)), _full((128, RW)),
                  _full((1, RW)), _full((128, RW)), _full((1, RW)), _full((1, RW)), _full((1, RW)),
                  _full((1, RW)), _full((1, RW)), _full((RW, RW)), _full((RW, D_MODEL))],
        out_specs=pl.BlockSpec((tc, D_MODEL), row),
        out_shape=jax.ShapeDtypeStruct((T, D_MODEL), F32),
        scratch_shapes=[pltpu.VMEM((NH, HD, HD), F32), pltpu.VMEM((tc, RW), F32)],
        compiler_params=pltpu.CompilerParams(dimension_semantics=("arbitrary", "arbitrary"),
                                             vmem_limit_bytes=VMEM_LIMIT),
        name="rwkv",
    )(pr, wup, w0, aup, a0, gup, k_k, k_a, r_k, ln_w, ln_b, bd, wo_bf)


def _s5_prep_body(ldt_r, are_r, aim_r, ldt_c, are_c, aim_c, bre_ref, bim_ref, lpr_ref, lpi_ref, bbr_ref, bbi_ref, *, tc):
    dt = jnp.exp(ldt_r[...])
    mag = jnp.exp(are_r[...] * dt)
    ang = aim_r[...] * dt
    lpr_ref[0:1, :] = mag * jnp.cos(ang)
    lpi_ref[0:1, :] = mag * jnp.sin(ang)
    n = 1
    while n < tc:
        pr = lpr_ref[n - 1:n, :]
        pi = lpi_ref[n - 1:n, :]
        qr = lpr_ref[0:n, :]
        qi = lpi_ref[0:n, :]
        lpr_ref[n:2 * n, :] = qr * pr - qi * pi
        lpi_ref[n:2 * n, :] = qr * pi + qi * pr
        n *= 2
    dtc = jnp.exp(ldt_c[...])
    are = are_c[...]
    aim = aim_c[...]
    magc = jnp.exp(are * dtc)
    angc = aim * dtc
    lre = magc * jnp.cos(angc)
    lim = magc * jnp.sin(angc)
    den = are * are + aim * aim
    nre = lre - 1.0
    fre = (nre * are + lim * aim) / den
    fim = (lim * are - nre * aim) / den
    bre = bre_ref[...]
    bim = bim_ref[...]
    bbr_ref[...] = fre * bre - fim * bim
    bbi_ref[...] = fre * bim + fim * bre


def _s5_prep(log_dt, a_re, a_im, b_re, b_im, tc):
    ng, ns = a_re.shape
    hh = b_re.shape[-1]
    n = ng * ns
    ldt = jnp.broadcast_to(log_dt[:, None], (ng, ns))
    row = lambda z: z.reshape(1, n)
    col = lambda z: z.reshape(n, 1)
    return pl.pallas_call(
        functools.partial(_s5_prep_body, tc=tc),
        out_shape=[jax.ShapeDtypeStruct((tc, n), F32), jax.ShapeDtypeStruct((tc, n), F32),
                   jax.ShapeDtypeStruct((n, hh), F32), jax.ShapeDtypeStruct((n, hh), F32)],
        name="s5_prep",
    )(row(ldt), row(a_re), row(a_im), col(ldt), col(a_re), col(a_im), b_re.reshape(n, hh), b_im.reshape(n, hh))


def _s5_body(u_ref, bw_ref, cw_ref, d_ref, lpr_ref, lpi_ref, wglu_ref, yb_ref, st_ref):
    @pl.when(pl.program_id(1) == 0)
    def _():
        st_ref[...] = jnp.zeros_like(st_ref)

    u = u_ref[...]
    tc = u.shape[0]
    nblk = bw_ref.shape[0]
    sb = S5_STATES // nblk
    ub = S5W // nblk
    bus = [_dot(u[:, j * ub:(j + 1) * ub], bw_ref[j], HI) for j in range(nblk)]
    xr = jnp.concatenate([bu[:, :sb] for bu in bus], axis=1)
    xi = jnp.concatenate([bu[:, sb:] for bu in bus], axis=1)
    row = lax.broadcasted_iota(I32, (tc, 1), 0)
    d = 1
    while d < tc:
        lr = lpr_ref[d - 1:d, :]
        li = lpi_ref[d - 1:d, :]
        keep = row >= d
        sr = jnp.where(keep, pltpu.roll(xr, d, 0), 0.0)
        si = jnp.where(keep, pltpu.roll(xi, d, 0), 0.0)
        xr, xi = xr + lr * sr - li * si, xi + lr * si + li * sr
        d *= 2
    pr = st_ref[0:1, :]
    pi = st_ref[1:2, :]
    lr = lpr_ref[...]
    li = lpi_ref[...]
    xr, xi = xr + lr * pr - li * pi, xi + lr * pi + li * pr
    st_ref[0:1, :] = xr[tc - 1:tc, :]
    st_ref[1:2, :] = xi[tc - 1:tc, :]
    ys = [_dot(jnp.concatenate([xr[:, j * sb:(j + 1) * sb], xi[:, j * sb:(j + 1) * sb]], axis=1), cw_ref[j], HI)
          for j in range(nblk)]
    y = jnp.concatenate(ys, axis=1) + d_ref[...] * u
    z = _dot(_gelu(y).astype(BF16), wglu_ref[...])
    yb_ref[...] = z[:, :D_MODEL] * _sigmoid(z[:, D_MODEL:])


def _s5(ps, bw, cw, dd, lpr, lpi, wglu_bf, nb, tiles, tc):
    T = ps.shape[0]
    row = lambda b, t: (b * tiles + t, 0)
    return pl.pallas_call(
        _s5_body,
        grid=(nb, tiles),
        in_specs=[pl.BlockSpec((tc, S5W), row), _full(bw.shape), _full(cw.shape), _full((1, S5W)),
                  _full((tc, S5_STATES)), _full((tc, S5_STATES)), _full((S5W, 2 * D_MODEL))],
        out_specs=pl.BlockSpec((tc, D_MODEL), row),
        out_shape=jax.ShapeDtypeStruct((T, D_MODEL), F32),
        scratch_shapes=[pltpu.VMEM((8, S5_STATES), F32)],
        compiler_params=pltpu.CompilerParams(dimension_semantics=("arbitrary", "arbitrary"),
                                             vmem_limit_bytes=VMEM_LIMIT),
        name="s5",
    )(ps, bw, cw, dd, lpr, lpi, wglu_bf)


def _stage2_layout():
    blocks = [("row", 0, 16), ("row", 1, 8), ("row", 2, 8), ("row", 3, 8),
              ("col", 0, 16), ("col", 1, 8), ("col", 2, 8)]
    flat = []
    for kind, a, n in blocks:
        for m in range(n):
            i, j = (a, m) if kind == "row" else (m, a)
            ok = (i + 1) * (j + 1) <= PEER_TOPK and (kind == "row" or i >= 4)
            flat.append(i * PEER_TOPK + j if ok else -1)
    return blocks, flat


def _topk_rows(vals, key, n_out, big):
    out_v, out_k = [], []
    for _ in range(n_out):
        m = jnp.max(vals, axis=0, keepdims=True)
        sel = jnp.min(jnp.where(vals == m, key, big), axis=0, keepdims=True)
        out_v.append(m)
        out_k.append(sel)
        vals = jnp.where(key == sel, -jnp.inf, vals)
    return out_v, out_k


def _mix_body(x_ref, ya_ref, yb_ref, g_ref, wout_ref, nf_ref, wq_ref, keys_ref, flat_ref,
              h_ref, xn_ref, ids_ref, gate_ref, sv_scr, si_scr):
    g = g_ref[...]
    mixed = g[:, :D_MODEL] * ya_ref[...] + g[:, D_MODEL:] * yb_ref[...]
    h = x_ref[...] + _dot(mixed.astype(BF16), wout_ref[...])
    h_ref[...] = h
    xn = h * lax.rsqrt(jnp.mean(h * h, axis=-1, keepdims=True) + EPS) * nf_ref[...]
    xn_ref[...] = xn
    q = _dot(xn, wq_ref[...], HI)
    tt = q.shape[0]
    key_iota = lax.broadcasted_iota(I32, (PEER_KEYS, tt), 0)
    blocks, _ = _stage2_layout()
    flat = flat_ref[...]
    flat_b = jnp.broadcast_to(flat, (flat.shape[0], tt))
    for hd in range(PEER_HEADS):
        for half in range(2):
            col = (hd * 2 + half) * PEER_HALF
            s_t = _dot_nt(keys_ref[hd * 2 + half], q[:, col:col + PEER_HALF])
            vs, ks = _topk_rows(s_t, key_iota, PEER_TOPK, PEER_KEYS)
            for i in range(PEER_TOPK):
                sv_scr[half, i:i + 1, :] = vs[i]
                si_scr[half, i:i + 1, :] = ks[i]
        sv1, sv2 = sv_scr[0], sv_scr[1]
        si1, si2 = si_scr[0], si_scr[1]
        cand, eid = [], []
        for kind, a, n in blocks:
            if kind == "row":
                cand.append(sv1[a:a + 1, :] + sv2[0:n, :])
                eid.append(si1[a:a + 1, :] * PEER_KEYS + si2[0:n, :])
            else:
                cand.append(sv1[0:n, :] + sv2[a:a + 1, :])
                eid.append(si1[0:n, :] * PEER_KEYS + si2[a:a + 1, :])
        cand = jnp.where(flat_b >= 0, jnp.concatenate(cand, axis=0), -jnp.inf)
        eid = jnp.concatenate(eid, axis=0)
        cv, cf = _topk_rows(cand, flat_b, PEER_TOPK, PEER_TOPK * PEER_TOPK)
        ex = [jnp.exp(c - cv[0]) for c in cv]
        tot = ex[0]
        for e in ex[1:]:
            tot = tot + e
        for i in range(PEER_TOPK):
            r = hd * PEER_TOPK + i
            ids_ref[r:r + 1, :] = jnp.sum(jnp.where(flat_b == cf[i], eid, 0), axis=0, keepdims=True)
            gate_ref[r:r + 1, :] = ex[i] / tot


def _mix(x2, ya, yb, g, wout_bf, norm_ffn, wq_bf, keys, flat, tt):
    T = x2.shape[0]
    row = lambda i: (i, 0)
    colb = lambda i: (0, i)
    nk = PEER_HEADS * PEER_TOPK
    return pl.pallas_call(
        _mix_body,
        grid=(T // tt,),
        in_specs=[pl.BlockSpec((tt, D_MODEL), row), pl.BlockSpec((tt, D_MODEL), row),
                  pl.BlockSpec((tt, D_MODEL), row), pl.BlockSpec((tt, 2 * D_MODEL), row),
                  _full((D_MODEL, D_MODEL)), _full((1, D_MODEL)), _full((D_MODEL, D_MODEL)),
                  _full(keys.shape), _full(flat.shape)],
        out_specs=[pl.BlockSpec((tt, D_MODEL), row), pl.BlockSpec((tt, D_MODEL), row),
                   pl.BlockSpec((nk, tt), colb), pl.BlockSpec((nk, tt), colb)],
        out_shape=[jax.ShapeDtypeStruct((T, D_MODEL), F32), jax.ShapeDtypeStruct((T, D_MODEL), F32),
                   jax.ShapeDtypeStruct((nk, T), I32), jax.ShapeDtypeStruct((nk, T), F32)],
        scratch_shapes=[pltpu.VMEM((2, PEER_TOPK, tt), F32), pltpu.VMEM((2, PEER_TOPK, tt), I32)],
        compiler_params=pltpu.CompilerParams(dimension_semantics=("arbitrary",),
                                             vmem_limit_bytes=VMEM_LIMIT),
        name="mix_topk",
    )(x2, ya, yb, g, wout_bf, norm_ffn, wq_bf, keys, flat)


ROW_WORDS = 4
GROUP = 8
HI_MASK = 0xFFFF0000


def _pack_table(tab):
    bits = lax.bitcast_convert_type(tab.astype(BF16), jnp.uint16).astype(U32)
    half = tab.shape[1] // 2
    packed = bits[:, :half] | (bits[:, half:] << 16)
    return packed.reshape(tab.shape[0] * ROW_WORDS, 128)


def _unpack(words):
    lo = pltpu.bitcast(words << 16, F32)
    hi = pltpu.bitcast(words & jnp.uint32(HI_MASK), F32)
    return lo, hi


def _peer_hidden_body(ids_ref, x_ref, gate_ref, tab_ref, c_ref, g_scr, hid_scr):
    nk = PEER_HEADS * PEER_TOPK
    tb = x_ref.shape[0]

    def group(gi, carry):
        base = pl.multiple_of(gi * GROUP, GROUP)
        for i in range(GROUP):
            t = base + i
            for mi in range(nk):
                e = ids_ref[t, mi]
                g_scr[mi * ROW_WORDS:(mi + 1) * ROW_WORDS, :] = tab_ref[pl.ds(pl.multiple_of(e * ROW_WORDS, ROW_WORDS), ROW_WORDS), :]
            xrow = x_ref[pl.ds(t, 1), :]
            acc = jnp.zeros((nk, 128), F32)
            for j in range(ROW_WORDS):
                lo, hi = _unpack(g_scr[pl.ds(j, nk, stride=ROW_WORDS), :])
                xa = xrow[:, j * 128:(j + 1) * 128]
                xb = xrow[:, 512 + j * 128:512 + (j + 1) * 128]
                acc = acc + lo * xa + hi * xb
            hid_scr[i:i + 1, :] = jnp.sum(acc.T, axis=0, keepdims=True)
        hid = hid_scr[...]
        c_ref[pl.ds(base, GROUP), :] = gate_ref[pl.ds(base, GROUP), :] * _gelu(hid)
        return carry

    lax.fori_loop(0, tb // GROUP, group, 0)


def _peer_hidden(ids, xn, gate, tab, tb):
    T = xn.shape[0]
    nk = PEER_HEADS * PEER_TOPK
    row = lambda i: (i, 0)
    return pl.pallas_call(
        _peer_hidden_body,
        grid=(T // tb,),
        in_specs=[pl.BlockSpec((tb, nk), row, memory_space=pltpu.SMEM), pl.BlockSpec((tb, D_MODEL), row),
                  pl.BlockSpec((tb, nk), row), _full(tab.shape)],
        out_specs=pl.BlockSpec((tb, nk), row),
        out_shape=jax.ShapeDtypeStruct((T, nk), F32),
        scratch_shapes=[pltpu.VMEM((nk * ROW_WORDS, 128), U32), pltpu.VMEM((GROUP, 128), F32)],
        compiler_params=pltpu.CompilerParams(dimension_semantics=("arbitrary",),
                                             vmem_limit_bytes=VMEM_LIMIT),
        name="peer_hidden",
    )(ids, xn, gate, tab)


def _peer_out_body(ids_ref, c_ref, tab_ref, o_ref):
    nk = PEER_HEADS * PEER_TOPK
    tb = o_ref.shape[0]

    def token(t, carry):
        acc_lo = jnp.zeros((ROW_WORDS, 128), F32)
        acc_hi = jnp.zeros((ROW_WORDS, 128), F32)
        for mi in range(nk):
            e = ids_ref[t, mi]
            w = c_ref[t, mi]
            lo, hi = _unpack(tab_ref[pl.ds(pl.multiple_of(e * ROW_WORDS, ROW_WORDS), ROW_WORDS), :])
            acc_lo = acc_lo + w * lo
            acc_hi = acc_hi + w * hi
        o_ref[t] = jnp.concatenate([acc_lo, acc_hi], axis=0)
        return carry

    lax.fori_loop(0, tb, token, 0)


def _peer_out(ids, c, tab, tb):
    T = ids.shape[0]
    nk = PEER_HEADS * PEER_TOPK
    row = lambda i: (i, 0)
    return pl.pallas_call(
        _peer_out_body,
        grid=(T // tb,),
        in_specs=[pl.BlockSpec((tb, nk), row, memory_space=pltpu.SMEM),
                  pl.BlockSpec((tb, nk), row, memory_space=pltpu.SMEM), _full(tab.shape)],
        out_specs=pl.BlockSpec((tb, 2 * ROW_WORDS, 128), lambda i: (i, 0, 0)),
        out_shape=jax.ShapeDtypeStruct((T, 2 * ROW_WORDS, 128), F32),
        compiler_params=pltpu.CompilerParams(dimension_semantics=("arbitrary",),
                                             vmem_limit_bytes=VMEM_LIMIT),
        name="peer_out",
    )(ids, c, tab)


def _final_body(h_ref, p_ref, gain_ref, o_ref):
    x = h_ref[...] + p_ref[...]
    o_ref[...] = x * lax.rsqrt(jnp.mean(x * x, axis=-1, keepdims=True) + EPS) * gain_ref[...]


def _final(h, p, gain, tt):
    T = h.shape[0]
    row = lambda i: (i, 0)
    return pl.pallas_call(
        _final_body,
        grid=(T // tt,),
        in_specs=[pl.BlockSpec((tt, D_MODEL), row), pl.BlockSpec((tt, D_MODEL), row), _full((1, D_MODEL))],
        out_specs=pl.BlockSpec((tt, D_MODEL), row),
        out_shape=jax.ShapeDtypeStruct((T, D_MODEL), F32),
        compiler_params=pltpu.CompilerParams(dimension_semantics=("arbitrary",)),
        name="final_norm",
    )(h, p, gain)


def _tile(n, pref):
    t = min(n, pref)
    assert n % t == 0, (n, t)
    return t


def kernel(x, norm_mix, w_in, b_gate, mu_rwkv, w_lora_up, w0, a_lora_up, a0, g_lora_up, k_k, k_a, r_k, ln_x_w, ln_x_b, w_o_rwkv, s5_log_dt, s5_a_re, s5_a_im, s5_b_re, s5_b_im, s5_c_re, s5_c_im, s5_d, w_glu_s5, w_out, norm_ffn, peer_wq, peer_subkeys, peer_u, peer_v, norm_final):
    B, S, D = x.shape
    assert D == D_MODEL and w_in.shape[0] == 1 and S % CHUNK == 0
    T = B * S
    x2 = x.reshape(T, D)
    r1 = lambda z: z.reshape(1, -1)

    tt = _tile(S, 256)
    pr, ps, g = _inproj(x2, r1(norm_mix[0]), w_in[0].astype(BF16), r1(b_gate[0]), r1(mu_rwkv[0]), B, S // tt, tt)

    zpad = jnp.zeros((64, RW), F32)
    wup = jnp.concatenate([w_lora_up[0], zpad], axis=0)
    aup = jnp.concatenate([zpad, a_lora_up[0]], axis=0)
    hid = jnp.arange(RW) // HD
    bd = (hid[:, None] == hid[None, :]).astype(BF16)
    tc = _tile(S, 256)
    ya = _rwkv(pr, wup, r1(w0[0]), aup, r1(a0[0]), g_lora_up[0], r1(k_k[0]), r1(k_a[0]), r1(r_k[0]),
               r1(ln_x_w[0]), r1(ln_x_b[0]), bd, w_o_rwkv[0].astype(BF16), B, S // tc, tc)

    ts = _tile(S, 256)
    lpr, lpi, bbr, bbi = _s5_prep(s5_log_dt[0], s5_a_re[0], s5_a_im[0], s5_b_re[0], s5_b_im[0], ts)
    ng, ns = s5_a_re[0].shape
    hh = s5_b_re.shape[-1]
    nblk, gb = 4, ng // 4
    eye_g = jnp.eye(gb, dtype=F32)

    def in_blocks(bb):
        return jnp.einsum("jgph,gk->jghkp", bb.reshape(nblk, gb, ns, hh), eye_g).reshape(nblk, gb * hh, gb * ns)

    def out_blocks(cc):
        return jnp.einsum("jghp,gk->jgpkh", cc.reshape(nblk, gb, hh, ns), eye_g).reshape(nblk, gb * ns, gb * hh)

    bw = jnp.concatenate([in_blocks(bbr), in_blocks(bbi)], axis=2)
    cw = jnp.concatenate([out_blocks(s5_c_re[0]), -out_blocks(s5_c_im[0])], axis=1)
    yb = _s5(ps, bw, cw, r1(s5_d[0]), lpr, lpi, w_glu_s5[0].astype(BF16), B, S // ts, ts)

    _, flat = _stage2_layout()
    flat = jnp.asarray(flat, I32).reshape(-1, 1)
    keys = peer_subkeys[0].reshape(PEER_HEADS * 2, PEER_KEYS, PEER_HALF)
    tm = _tile(T, 256)
    h, xn, ids_t, gate_t = _mix(x2, ya, yb, g, w_out[0].astype(BF16), r1(norm_ffn[0]), peer_wq[0],
                                keys, flat, tm)
    ids = ids_t.T
    gate = gate_t.T

    tb = _tile(T, 256)
    c = _peer_hidden(ids, xn, gate, _pack_table(peer_u[0]), tb)
    po = _peer_out(ids, c, _pack_table(peer_v[0]), tb).reshape(T, D)
    out = _final(h, po, r1(norm_final), _tile(T, 512))
    return out.reshape(B, S, D)
```

```python
import functools
import math

import jax
import jax.numpy as jnp
from jax import lax
from jax.experimental import pallas as pl
from jax.experimental.pallas import tpu as pltpu

F32 = jnp.float32
BF16 = jnp.bfloat16
I32 = jnp.int32
U32 = jnp.uint32
HI = lax.Precision.HIGHEST

EPS = 1e-6
GN_EPS = 64e-5
D_MODEL = 1024
RW = 512
NH = 8
HD = 64
N_RWKV_COLS = 1792
S5W = 512
S5_STATES = 2048
PEER_HEADS = 8
PEER_KEYS = 128
PEER_TOPK = 16
PEER_HALF = 64
CHUNK = 64

VMEM_LIMIT = 56 * 1024 * 1024


def _dot(a, b, prec=None):
    return jnp.dot(a, b, preferred_element_type=F32, precision=prec)


def _dot_nt(a, b, prec=HI):
    return lax.dot_general(a, b, (((1,), (1,)), ((), ())), preferred_element_type=F32, precision=prec)


def _dot_tn(a, b, prec=HI):
    return lax.dot_general(a, b, (((0,), (0,)), ((), ())), preferred_element_type=F32, precision=prec)


_NN = (((1,), (0,)), ((), ()))
_NT = (((1,), (1,)), ((), ()))
_TN = (((0,), (0,)), ((), ()))


def _split(x):
    hi = x.astype(BF16)
    return hi, (x - hi.astype(F32)).astype(BF16)


def _dot3(a, b, dims=_NN):
    dg = lambda p, q: lax.dot_general(p, q, dims, preferred_element_type=F32)
    return dg(a[0], b[0]) + (dg(a[0], b[1]) + dg(a[1], b[0]))


def _dot2(a, b, dims=_NN):
    dg = lambda p, q: lax.dot_general(p, q, dims, preferred_element_type=F32)
    return dg(a[0], b) + dg(a[1], b)


def _sigmoid(x):
    return 1.0 / (1.0 + jnp.exp(-x))


def _softplus(x):
    return jnp.maximum(x, 0.0) + jnp.log(1.0 + jnp.exp(-jnp.abs(x)))


def _gelu(x):
    return 0.5 * x * (1.0 + jnp.tanh(math.sqrt(2.0 / math.pi) * (x + 0.044715 * (x * x * x))))


def _full(shape):
    n = len(shape)
    return pl.BlockSpec(shape, lambda *_: (0,) * n)


def _inproj_body(x_ref, gain_ref, w_ref, bg_ref, mu_ref, pr_ref, ps_ref, g_ref, prev_ref):
    @pl.when(pl.program_id(1) == 0)
    def _():
        prev_ref[...] = jnp.zeros_like(prev_ref)

    x = x_ref[...]
    xn = x * lax.rsqrt(jnp.mean(x * x, axis=-1, keepdims=True) + EPS) * gain_ref[...]
    xb = xn.astype(BF16)
    p = _dot(xb, w_ref[:, :N_RWKV_COLS])
    tt = p.shape[0]
    row = lax.broadcasted_iota(I32, p.shape, 0)
    prev = jnp.broadcast_to(prev_ref[0:1, :], p.shape)
    shifted = jnp.where(row == 0, prev, pltpu.roll(p, 1, 0))
    prev_ref[0:1, :] = p[tt - 1:tt, :]
    pr_ref[...] = p + (shifted - p) * mu_ref[...]
    ps_ref[...] = _dot(xb, w_ref[:, N_RWKV_COLS:N_RWKV_COLS + S5W])
    g_ref[...] = _sigmoid(_dot(xb, w_ref[:, N_RWKV_COLS + S5W:]) + bg_ref[...])


def _inproj(x2, gain, w_in_bf, b_gate, mu, nb, tiles, tt):
    T = x2.shape[0]
    ncols = w_in_bf.shape[1]
    row = lambda b, t: (b * tiles + t, 0)
    return pl.pallas_call(
        _inproj_body,
        grid=(nb, tiles),
        in_specs=[pl.BlockSpec((tt, D_MODEL), row), _full((1, D_MODEL)), _full((D_MODEL, ncols)),
                  _full((1, 2 * D_MODEL)), _full((1, N_RWKV_COLS))],
        out_specs=[pl.BlockSpec((tt, N_RWKV_COLS), row), pl.BlockSpec((tt, S5W), row),
                   pl.BlockSpec((tt, 2 * D_MODEL), row)],
        out_shape=[jax.ShapeDtypeStruct((T, N_RWKV_COLS), F32), jax.ShapeDtypeStruct((T, S5W), F32),
                   jax.ShapeDtypeStruct((T, 2 * D_MODEL), F32)],
        scratch_shapes=[pltpu.VMEM((8, N_RWKV_COLS), F32)],
        compiler_params=pltpu.CompilerParams(dimension_semantics=("arbitrary", "arbitrary"),
                                             vmem_limit_bytes=VMEM_LIMIT),
        name="inproj",
    )(x2, gain, w_in_bf, b_gate, mu)


def _rwkv_body(pr_ref, wup_ref, w0_ref, aup_ref, a0_ref, gup_ref, kk_ref, ka_ref, rk_ref, lnw_ref, lnb_ref,
               bd_ref, wo_ref, ya_ref, st_ref, y_scr, *, nc):
    @pl.when(pl.program_id(1) == 0)
    def _():
        st_ref[...] = jnp.zeros_like(st_ref)

    L = CHUNK
    r = pr_ref[:, 0:RW]
    k = pr_ref[:, RW:2 * RW]
    v = pr_ref[:, 2 * RW:3 * RW]
    x128 = pr_ref[:, 3 * RW:3 * RW + 128]
    xg = pr_ref[:, 3 * RW + 128:3 * RW + 256]
    bd = bd_ref[...]
    seg = lambda z: _dot2(_split(z), bd)

    w_log = -_softplus(-(w0_ref[...] + _dot(jnp.tanh(x128), wup_ref[...], HI))) - 0.5
    logw = -jnp.exp(w_log)
    a_lr = _sigmoid(a0_ref[...] + _dot(x128, aup_ref[...], HI))
    g = _dot(_sigmoid(xg), gup_ref[...], HI)
    kk = k * kk_ref[...]
    k2 = k * (1.0 + (a_lr - 1.0) * ka_ref[...])
    kk = kk / jnp.maximum(jnp.sqrt(seg(kk * kk)), 1e-12)
    b = kk * a_lr
    am = -kk

    ri = lax.broadcasted_iota(I32, (L, L), 0)
    ci = lax.broadcasted_iota(I32, (L, L), 1)
    strict = ri > ci
    incl = ri >= ci
    eye = (ri == ci).astype(F32)
    tri = incl.astype(BF16)
    quad = []
    bit = 0
    while (1 << bit) < L:
        s = 1 << bit
        quad.append((((ri >> (bit + 1)) == (ci >> (bit + 1))) & ((ri & s) != 0) & ((ci & s) == 0)).astype(F32))
        bit += 1
    sl = [slice(h * HD, (h + 1) * HD) for h in range(NH)]
    hr = range(NH)

    m_all, cc_all, r2_all, y0_all = [], [], [], []
    for c_i in range(nc):
        rows = slice(c_i * L, (c_i + 1) * L)
        lw = logw[rows]
        lw_h = lw.astype(BF16)
        lw_r = lw - lw_h.astype(F32)
        lw_m = lw_r.astype(BF16)
        lw_l = (lw_r - lw_m.astype(F32)).astype(BF16)
        c = _dot(tri, lw_h) + (_dot(tri, lw_m) + _dot(tri, lw_l))
        cl = c[L - 1:L, :]
        e_c = jnp.exp(c)
        e_nc = jnp.exp(-c)
        e_cp = jnp.exp(c - lw)
        e_rem = jnp.exp(cl - c)
        e_last = jnp.exp(cl)
        at_all = am[rows] * e_cp
        rt_all = r[rows] * e_c
        bt_all = b[rows] * e_nc
        kt_all = k2[rows] * e_nc
        bh_all = b[rows] * e_rem
        kh_all = k2[rows] * e_rem
        v_all = v[rows]
        at = [at_all[:, s] for s in sl]
        rt = [rt_all[:, s] for s in sl]
        q = [_dot3(_split(jnp.concatenate([at[h], rt[h]], axis=0)),
                   _split(jnp.concatenate([bt_all[:, sl[h]], kt_all[:, sl[h]]], axis=0)), _NT) for h in hr]
        n_ab = [jnp.where(strict, q[h][:L, :L], 0.0) for h in hr]
        a_kr = [_split(jnp.concatenate([jnp.where(strict, q[h][:L, L:], 0.0),
                                        jnp.where(incl, q[h][L:, L:], 0.0)], axis=0)) for h in hr]
        a_rb = [_split(jnp.where(incl, q[h][L:, :L], 0.0)) for h in hr]
        dinv = [eye + n_ab[h] * quad[0] for h in hr]
        for qm in quad[1:]:
            ds = [_split(dinv[h]) for h in hr]
            t1 = [_dot3(_split(n_ab[h] * qm), ds[h]) for h in hr]
            dinv = [dinv[h] + _dot3(ds[h], _split(t1[h])) for h in hr]
        ds = [_split(dinv[h]) for h in hr]
        vs = [_split(v_all[:, s]) for s in sl]
        av = [_dot3(a_kr[h], vs[h]) for h in hr]
        at2 = [_dot3(ds[h], _split(at[h])) for h in hr]
        u0 = [_dot3(ds[h], _split(av[h][:L])) for h in hr]
        at2s = [_split(at2[h]) for h in hr]
        u0s = [_split(u0[h]) for h in hr]
        bhs = [_split(bh_all[:, s]) for s in sl]
        khs = [_split(kh_all[:, s]) for s in sl]
        m_all.append([eye * e_last[:, sl[h]] + _dot3(_split(at2[h].T), bhs[h]) for h in hr])
        cc_all.append([_dot3(_split(u0[h].T), bhs[h]) + _dot3(_split(v_all[:, sl[h]].T), khs[h]) for h in hr])
        r2_all.append([_split(rt[h] + _dot3(a_rb[h], at2s[h])) for h in hr])
        y0_all.append([_dot3(a_rb[h], u0s[h]) + av[h][L:] for h in hr])

    st = [st_ref[h] for h in hr]
    for c_i in range(nc):
        rows = slice(c_i * L, (c_i + 1) * L)
        ss = [_split(st[h]) for h in hr]
        for h in hr:
            y_scr[rows, sl[h]] = y0_all[c_i][h] + _dot3(r2_all[c_i][h], ss[h], _NT)
        st = [_dot3(ss[h], _split(m_all[c_i][h])) + cc_all[c_i][h] for h in hr]
    for h in hr:
        st_ref[h] = st[h]

    y = y_scr[...]
    mean = seg(y) * (1.0 / HD)
    yc = y - mean
    var = seg(yc * yc) * (1.0 / HD)
    yn = yc * lax.rsqrt(var + GN_EPS) * lnw_ref[...] + lnb_ref[...]
    bonus = seg(r * k2 * rk_ref[...]) * v
    out = (yn + bonus) * g
    ya_ref[...] = _dot(out.astype(BF16), wo_ref[...])


def _rwkv(pr, wup, w0, aup, a0, gup, k_k, k_a, r_k, ln_w, ln_b, bd, wo_bf, nb, tiles, tc):
    T = pr.shape[0]
    row = lambda b, t: (b * tiles + t, 0)
    return pl.pallas_call(
        functools.partial(_rwkv_body, nc=tc // CHUNK),
        grid=(nb, tiles),
        in_specs=[pl.BlockSpec((tc, N_RWKV_COLS), row), _full((128, RW)), _full((1, RW)), _full((128, RW)),
                  _full((1, RW)), _full((128, RW)), _full((1, RW)), _full((1, RW)), _full((1, RW)),
                  _full((1, RW)), _full((1, RW)), _full((RW, RW)), _full((RW, D_MODEL))],
        out_specs=pl.BlockSpec((tc, D_MODEL), row),
        out_shape=jax.ShapeDtypeStruct((T, D_MODEL), F32),
        scratch_shapes=[pltpu.VMEM((NH, HD, HD), F32), pltpu.VMEM((tc, RW), F32)],
        compiler_params=pltpu.CompilerParams(dimension_semantics=("arbitrary", "arbitrary"),
                                             vmem_limit_bytes=VMEM_LIMIT),
        name="rwkv",
    )(pr, wup, w0, aup, a0, gup, k_k, k_a, r_k, ln_w, ln_b, bd, wo_bf)


def _s5_prep_body(ldt_r, are_r, aim_r, ldt_c, are_c, aim_c, bre_ref, bim_ref, lpr_ref, lpi_ref, bbr_ref, bbi_ref, *, tc):
    dt = jnp.exp(ldt_r[...])
    mag = jnp.exp(are_r[...] * dt)
    ang = aim_r[...] * dt
    lpr_ref[0:1, :] = mag * jnp.cos(ang)
    lpi_ref[0:1, :] = mag * jnp.sin(ang)
    n = 1
    while n < tc:
        pr = lpr_ref[n - 1:n, :]
        pi = lpi_ref[n - 1:n, :]
        qr = lpr_ref[0:n, :]
        qi = lpi_ref[0:n, :]
        lpr_ref[n:2 * n, :] = qr * pr - qi * pi
        lpi_ref[n:2 * n, :] = qr * pi + qi * pr
        n *= 2
    dtc = jnp.exp(ldt_c[...])
    are = are_c[...]
    aim = aim_c[...]
    magc = jnp.exp(are * dtc)
    angc = aim * dtc
    lre = magc * jnp.cos(angc)
    lim = magc * jnp.sin(angc)
    den = are * are + aim * aim
    nre = lre - 1.0
    fre = (nre * are + lim * aim) / den
    fim = (lim * are - nre * aim) / den
    bre = bre_ref[...]
    bim = bim_ref[...]
    bbr_ref[...] = fre * bre - fim * bim
    bbi_ref[...] = fre * bim + fim * bre


def _s5_prep(log_dt, a_re, a_im, b_re, b_im, tc):
    ng, ns = a_re.shape
    hh = b_re.shape[-1]
    n = ng * ns
    ldt = jnp.broadcast_to(log_dt[:, None], (ng, ns))
    row = lambda z: z.reshape(1, n)
    col = lambda z: z.reshape(n, 1)
    return pl.pallas_call(
        functools.partial(_s5_prep_body, tc=tc),
        out_shape=[jax.ShapeDtypeStruct((tc, n), F32), jax.ShapeDtypeStruct((tc, n), F32),
                   jax.ShapeDtypeStruct((n, hh), F32), jax.ShapeDtypeStruct((n, hh), F32)],
        name="s5_prep",
    )(row(ldt), row(a_re), row(a_im), col(ldt), col(a_re), col(a_im), b_re.reshape(n, hh), b_im.reshape(n, hh))


def _s5_body(u_ref, bw_ref, cw_ref, d_ref, lpr_ref, lpi_ref, wglu_ref, yb_ref, st_ref):
    @pl.when(pl.program_id(1) == 0)
    def _():
        st_ref[...] = jnp.zeros_like(st_ref)

    u = u_ref[...]
    tc = u.shape[0]
    nblk = bw_ref.shape[0]
    sb = S5_STATES // nblk
    ub = S5W // nblk
    bus = [_dot(u[:, j * ub:(j + 1) * ub], bw_ref[j], HI) for j in range(nblk)]
    xr = jnp.concatenate([bu[:, :sb] for bu in bus], axis=1)
    xi = jnp.concatenate([bu[:, sb:] for bu in bus], axis=1)
    row = lax.broadcasted_iota(I32, (tc, 1), 0)
    d = 1
    while d < tc:
        lr = lpr_ref[d - 1:d, :]
        li = lpi_ref[d - 1:d, :]
        keep = row >= d
        sr = jnp.where(keep, pltpu.roll(xr, d, 0), 0.0)
        si = jnp.where(keep, pltpu.roll(xi, d, 0), 0.0)
        xr, xi = xr + lr * sr - li * si, xi + lr * si + li * sr
        d *= 2
    pr = st_ref[0:1, :]
    pi = st_ref[1:2, :]
    lr = lpr_ref[...]
    li = lpi_ref[...]
    xr, xi = xr + lr * pr - li * pi, xi + lr * pi + li * pr
    st_ref[0:1, :] = xr[tc - 1:tc, :]
    st_ref[1:2, :] = xi[tc - 1:tc, :]
    ys = [_dot(jnp.concatenate([xr[:, j * sb:(j + 1) * sb], xi[:, j * sb:(j + 1) * sb]], axis=1), cw_ref[j], HI)
          for j in range(nblk)]
    y = jnp.concatenate(ys, axis=1) + d_ref[...] * u
    z = _dot(_gelu(y).astype(BF16), wglu_ref[...])
    yb_ref[...] = z[:, :D_MODEL] * _sigmoid(z[:, D_MODEL:])


def _s5(ps, bw, cw, dd, lpr, lpi, wglu_bf, nb, tiles, tc):
    T = ps.shape[0]
    row = lambda b, t: (b * tiles + t, 0)
    return pl.pallas_call(
        _s5_body,
        grid=(nb, tiles),
        in_specs=[pl.BlockSpec((tc, S5W), row), _full(bw.shape), _full(cw.shape), _full((1, S5W)),
                  _full((tc, S5_STATES)), _full((tc, S5_STATES)), _full((S5W, 2 * D_MODEL))],
        out_specs=pl.BlockSpec((tc, D_MODEL), row),
        out_shape=jax.ShapeDtypeStruct((T, D_MODEL), F32),
        scratch_shapes=[pltpu.VMEM((8, S5_STATES), F32)],
        compiler_params=pltpu.CompilerParams(dimension_semantics=("arbitrary", "arbitrary"),
                                             vmem_limit_bytes=VMEM_LIMIT),
        name="s5",
    )(ps, bw, cw, dd, lpr, lpi, wglu_bf)


def _stage2_layout():
    blocks = [("row", 0, 16), ("row", 1, 8), ("row", 2, 8), ("row", 3, 8),
              ("col", 0, 16), ("col", 1, 8), ("col", 2, 8)]
    flat = []
    for kind, a, n in blocks:
        for m in range(n):
            i, j = (a, m) if kind == "row" else (m, a)
            ok = (i + 1) * (j + 1) <= PEER_TOPK and (kind == "row" or i >= 4)
            flat.append(i * PEER_TOPK + j if ok else -1)
    return blocks, flat


def _topk_rows(vals, key, n_out, big):
    out_v, out_k = [], []
    for _ in range(n_out):
        m = jnp.max(vals, axis=0, keepdims=True)
        sel = jnp.min(jnp.where(vals == m, key, big), axis=0, keepdims=True)
        out_v.append(m)
        out_k.append(sel)
        vals = jnp.where(key == sel, -jnp.inf, vals)
    return out_v, out_k


def _mix_body(x_ref, ya_ref, yb_ref, g_ref, wout_ref, nf_ref, wq_ref, keys_ref, flat_ref,
              h_ref, xn_ref, ids_ref, gate_ref, sv_scr, si_scr):
    g = g_ref[...]
    mixed = g[:, :D_MODEL] * ya_ref[...] + g[:, D_MODEL:] * yb_ref[...]
    h = x_ref[...] + _dot(mixed.astype(BF16), wout_ref[...])
    h_ref[...] = h
    xn = h * lax.rsqrt(jnp.mean(h * h, axis=-1, keepdims=True) + EPS) * nf_ref[...]
    xn_ref[...] = xn
    q = _dot(xn, wq_ref[...], HI)
    tt = q.shape[0]
    key_iota = lax.broadcasted_iota(I32, (PEER_KEYS, tt), 0)
    blocks, _ = _stage2_layout()
    flat = flat_ref[...]
    flat_b = jnp.broadcast_to(flat, (flat.shape[0], tt))
    for hd in range(PEER_HEADS):
        for half in range(2):
            col = (hd * 2 + half) * PEER_HALF
            s_t = _dot_nt(keys_ref[hd * 2 + half], q[:, col:col + PEER_HALF])
            vs, ks = _topk_rows(s_t, key_iota, PEER_TOPK, PEER_KEYS)
            for i in range(PEER_TOPK):
                sv_scr[half, i:i + 1, :] = vs[i]
                si_scr[half, i:i + 1, :] = ks[i]
        sv1, sv2 = sv_scr[0], sv_scr[1]
        si1, si2 = si_scr[0], si_scr[1]
        cand, eid = [], []
        for kind, a, n in blocks:
            if kind == "row":
                cand.append(sv1[a:a + 1, :] + sv2[0:n, :])
                eid.append(si1[a:a + 1, :] * PEER_KEYS + si2[0:n, :])
            else:
                cand.append(sv1[0:n, :] + sv2[a:a + 1, :])
                eid.append(si1[0:n, :] * PEER_KEYS + si2[a:a + 1, :])
        cand = jnp.where(flat_b >= 0, jnp.concatenate(cand, axis=0), -jnp.inf)
        eid = jnp.concatenate(eid, axis=0) * ROW_WORDS
        cv, cf = _topk_rows(cand, flat_b, PEER_TOPK, PEER_TOPK * PEER_TOPK)
        ex = [jnp.exp(c - cv[0]) for c in cv]
        tot = ex[0]
        for e in ex[1:]:
            tot = tot + e
        for i in range(PEER_TOPK):
            r = hd * PEER_TOPK + i
            ids_ref[r:r + 1, :] = jnp.sum(jnp.where(flat_b == cf[i], eid, 0), axis=0, keepdims=True)
            gate_ref[r:r + 1, :] = ex[i] / tot


def _mix(x2, ya, yb, g, wout_bf, norm_ffn, wq_bf, keys, flat, tt):
    T = x2.shape[0]
    row = lambda i: (i, 0)
    colb = lambda i: (0, i)
    nk = PEER_HEADS * PEER_TOPK
    return pl.pallas_call(
        _mix_body,
        grid=(T // tt,),
        in_specs=[pl.BlockSpec((tt, D_MODEL), row), pl.BlockSpec((tt, D_MODEL), row),
                  pl.BlockSpec((tt, D_MODEL), row), pl.BlockSpec((tt, 2 * D_MODEL), row),
                  _full((D_MODEL, D_MODEL)), _full((1, D_MODEL)), _full((D_MODEL, D_MODEL)),
                  _full(keys.shape), _full(flat.shape)],
        out_specs=[pl.BlockSpec((tt, D_MODEL), row), pl.BlockSpec((tt, D_MODEL), row),
                   pl.BlockSpec((nk, tt), colb), pl.BlockSpec((nk, tt), colb)],
        out_shape=[jax.ShapeDtypeStruct((T, D_MODEL), F32), jax.ShapeDtypeStruct((T, D_MODEL), F32),
                   jax.ShapeDtypeStruct((nk, T), I32), jax.ShapeDtypeStruct((nk, T), F32)],
        scratch_shapes=[pltpu.VMEM((2, PEER_TOPK, tt), F32), pltpu.VMEM((2, PEER_TOPK, tt), I32)],
        compiler_params=pltpu.CompilerParams(dimension_semantics=("arbitrary",),
                                             vmem_limit_bytes=VMEM_LIMIT),
        name="mix_topk",
    )(x2, ya, yb, g, wout_bf, norm_ffn, wq_bf, keys, flat)


ROW_WORDS = 4
GROUP = 8
HI_MASK = 0xFFFF0000


def _pack_table(tab):
    bits = lax.bitcast_convert_type(tab.astype(BF16), jnp.uint16).astype(U32)
    half = tab.shape[1] // 2
    packed = bits[:, :half] | (bits[:, half:] << 16)
    return packed.reshape(tab.shape[0] * ROW_WORDS, 128)


def _unpack(words):
    lo = pltpu.bitcast(words << 16, F32)
    hi = pltpu.bitcast(words & jnp.uint32(HI_MASK), F32)
    return lo, hi


def _peer_hidden_body(ids_ref, x_ref, gate_ref, tab_ref, c_ref, g_scr, hid_scr):
    nk = PEER_HEADS * PEER_TOPK
    tb = x_ref.shape[0]

    def group(gi, carry):
        base = pl.multiple_of(gi * GROUP, GROUP)
        for i in range(GROUP):
            t = base + i
            ids_t = ids_ref.at[t]
            for mi in range(nk):
                e4 = pl.multiple_of(ids_t[mi], ROW_WORDS)
                g_scr[mi * ROW_WORDS:(mi + 1) * ROW_WORDS, :] = tab_ref[pl.ds(e4, ROW_WORDS), :]
            xrow = x_ref[pl.ds(t, 1), :]
            acc = jnp.zeros((nk, 128), F32)
            for j in range(ROW_WORDS):
                lo, hi = _unpack(g_scr[pl.ds(j, nk, stride=ROW_WORDS), :])
                xa = xrow[:, j * 128:(j + 1) * 128]
                xb = xrow[:, 512 + j * 128:512 + (j + 1) * 128]
                acc = acc + lo * xa + hi * xb
            hid_scr[i:i + 1, :] = jnp.sum(acc.T, axis=0, keepdims=True)
        rows = pl.ds(base, GROUP)
        c_ref[rows, :] = gate_ref[rows, :] * _gelu(hid_scr[...])
        return carry

    lax.fori_loop(0, tb // GROUP, group, 0)


def _peer_hidden(ids4, xn, gate, tab, tb):
    T = xn.shape[0]
    nk = PEER_HEADS * PEER_TOPK
    row = lambda i: (i, 0)
    return pl.pallas_call(
        _peer_hidden_body,
        grid=(T // tb,),
        in_specs=[pl.BlockSpec((tb, nk), row, memory_space=pltpu.SMEM),
                  pl.BlockSpec((tb, D_MODEL), row), pl.BlockSpec((tb, nk), row), _full(tab.shape)],
        out_specs=pl.BlockSpec((tb, nk), row),
        out_shape=jax.ShapeDtypeStruct((T, nk), F32),
        scratch_shapes=[pltpu.VMEM((nk * ROW_WORDS, 128), U32), pltpu.VMEM((GROUP, 128), F32)],
        compiler_params=pltpu.CompilerParams(dimension_semantics=("arbitrary",),
                                             vmem_limit_bytes=VMEM_LIMIT),
        name="peer_hidden",
    )(ids4, xn, gate, tab)


def _peer_out_body(ids_ref, c_ref, tab_ref, o_ref):
    nk = PEER_HEADS * PEER_TOPK
    tb = o_ref.shape[0]

    def token(t, carry):
        ids_t = ids_ref.at[t]
        c_t = c_ref.at[t]
        acc_lo = jnp.zeros((ROW_WORDS, 128), F32)
        acc_hi = jnp.zeros((ROW_WORDS, 128), F32)
        for mi in range(nk):
            e4 = pl.multiple_of(ids_t[mi], ROW_WORDS)
            w = c_t[mi]
            lo, hi = _unpack(tab_ref[pl.ds(e4, ROW_WORDS), :])
            acc_lo = acc_lo + w * lo
            acc_hi = acc_hi + w * hi
        o_ref[t] = jnp.concatenate([acc_lo, acc_hi], axis=0)
        return carry

    lax.fori_loop(0, tb, token, 0)


def _peer_out(ids4, c, tab, tb):
    T = ids4.shape[0]
    nk = PEER_HEADS * PEER_TOPK
    row = lambda i: (i, 0)
    return pl.pallas_call(
        _peer_out_body,
        grid=(T // tb,),
        in_specs=[pl.BlockSpec((tb, nk), row, memory_space=pltpu.SMEM),
                  pl.BlockSpec((tb, nk), row, memory_space=pltpu.SMEM), _full(tab.shape)],
        out_specs=pl.BlockSpec((tb, 2 * ROW_WORDS, 128), lambda i: (i, 0, 0)),
        out_shape=jax.ShapeDtypeStruct((T, 2 * ROW_WORDS, 128), F32),
        compiler_params=pltpu.CompilerParams(dimension_semantics=("arbitrary",),
                                             vmem_limit_bytes=VMEM_LIMIT),
        name="peer_out",
    )(ids4, c, tab)


def _final_body(h_ref, p_ref, gain_ref, o_ref):
    x = h_ref[...] + p_ref[...]
    o_ref[...] = x * lax.rsqrt(jnp.mean(x * x, axis=-1, keepdims=True) + EPS) * gain_ref[...]


def _final(h, p, gain, tt):
    T = h.shape[0]
    row = lambda i: (i, 0)
    return pl.pallas_call(
        _final_body,
        grid=(T // tt,),
        in_specs=[pl.BlockSpec((tt, D_MODEL), row), pl.BlockSpec((tt, D_MODEL), row), _full((1, D_MODEL))],
        out_specs=pl.BlockSpec((tt, D_MODEL), row),
        out_shape=jax.ShapeDtypeStruct((T, D_MODEL), F32),
        compiler_params=pltpu.CompilerParams(dimension_semantics=("arbitrary",)),
        name="final_norm",
    )(h, p, gain)


def _tile(n, pref):
    t = min(n, pref)
    assert n % t == 0, (n, t)
    return t


def kernel(x, norm_mix, w_in, b_gate, mu_rwkv, w_lora_up, w0, a_lora_up, a0, g_lora_up, k_k, k_a, r_k, ln_x_w, ln_x_b, w_o_rwkv, s5_log_dt, s5_a_re, s5_a_im, s5_b_re, s5_b_im, s5_c_re, s5_c_im, s5_d, w_glu_s5, w_out, norm_ffn, peer_wq, peer_subkeys, peer_u, peer_v, norm_final):
    B, S, D = x.shape
    assert D == D_MODEL and w_in.shape[0] == 1 and S % CHUNK == 0
    T = B * S
    x2 = x.reshape(T, D)
    r1 = lambda z: z.reshape(1, -1)

    tt = _tile(S, 256)
    pr, ps, g = _inproj(x2, r1(norm_mix[0]), w_in[0].astype(BF16), r1(b_gate[0]), r1(mu_rwkv[0]), B, S // tt, tt)

    zpad = jnp.zeros((64, RW), F32)
    wup = jnp.concatenate([w_lora_up[0], zpad], axis=0)
    aup = jnp.concatenate([zpad, a_lora_up[0]], axis=0)
    hid = jnp.arange(RW) // HD
    bd = (hid[:, None] == hid[None, :]).astype(BF16)
    tc = _tile(S, 256)
    ya = _rwkv(pr, wup, r1(w0[0]), aup, r1(a0[0]), g_lora_up[0], r1(k_k[0]), r1(k_a[0]), r1(r_k[0]),
               r1(ln_x_w[0]), r1(ln_x_b[0]), bd, w_o_rwkv[0].astype(BF16), B, S // tc, tc)

    ts = _tile(S, 256)
    lpr, lpi, bbr, bbi = _s5_prep(s5_log_dt[0], s5_a_re[0], s5_a_im[0], s5_b_re[0], s5_b_im[0], ts)
    ng, ns = s5_a_re[0].shape
    hh = s5_b_re.shape[-1]
    nblk, gb = 4, ng // 4
    eye_g = jnp.eye(gb, dtype=F32)

    def in_blocks(bb):
        return jnp.einsum("jgph,gk->jghkp", bb.reshape(nblk, gb, ns, hh), eye_g).reshape(nblk, gb * hh, gb * ns)

    def out_blocks(cc):
        return jnp.einsum("jghp,gk->jgpkh", cc.reshape(nblk, gb, hh, ns), eye_g).reshape(nblk, gb * ns, gb * hh)

    bw = jnp.concatenate([in_blocks(bbr), in_blocks(bbi)], axis=2)
    cw = jnp.concatenate([out_blocks(s5_c_re[0]), -out_blocks(s5_c_im[0])], axis=1)
    yb = _s5(ps, bw, cw, r1(s5_d[0]), lpr, lpi, w_glu_s5[0].astype(BF16), B, S // ts, ts)

    _, flat = _stage2_layout()
    flat = jnp.asarray(flat, I32).reshape(-1, 1)
    keys = peer_subkeys[0].reshape(PEER_HEADS * 2, PEER_KEYS, PEER_HALF)
    tm = _tile(T, 256)
    h, xn, ids_t, gate_t = _mix(x2, ya, yb, g, w_out[0].astype(BF16), r1(norm_ffn[0]), peer_wq[0],
                                keys, flat, tm)
    ids4 = ids_t.T
    gate = gate_t.T

    tb = _tile(T, 256)
    c = _peer_hidden(ids4, xn, gate, _pack_table(peer_u[0]), tb)
    po = _peer_out(ids4, c, _pack_table(peer_v[0]), tb).reshape(T, D)
    out = _final(h, po, r1(norm_final), _tile(T, 512))
    return out.reshape(B, S, D)
```

```python
import functools
import math

import jax
import jax.numpy as jnp
from jax import lax
from jax.experimental import pallas as pl
from jax.experimental.pallas import tpu as pltpu

F32 = jnp.float32
BF16 = jnp.bfloat16
I32 = jnp.int32
HI = lax.Precision.HIGHEST

EPS = 1e-6
GN_EPS = 64e-5
D_MODEL = 1024
RW = 512
NH = 8
HD = 64
N_RWKV_COLS = 1792
S5W = 512
S5_STATES = 2048
PEER_HEADS = 8
PEER_KEYS = 128
PEER_TOPK = 16
PEER_HALF = 64
CHUNK = 64

VMEM_LIMIT = 56 * 1024 * 1024


def _dot(a, b, prec=None):
    return jnp.dot(a, b, preferred_element_type=F32, precision=prec)


def _dot_nt(a, b, prec=HI):
    return lax.dot_general(a, b, (((1,), (1,)), ((), ())), preferred_element_type=F32, precision=prec)


_NN = (((1,), (0,)), ((), ()))
_NT = (((1,), (1,)), ((), ()))
_TN = (((0,), (0,)), ((), ()))


def _split(x):
    hi = x.astype(BF16)
    return hi, (x - hi.astype(F32)).astype(BF16)


def _dot3(a, b, dims=_NN):
    dg = lambda p, q: lax.dot_general(p, q, dims, preferred_element_type=F32)
    return dg(a[0], b[0]) + (dg(a[0], b[1]) + dg(a[1], b[0]))


def _dot2(a, b, dims=_NN):
    dg = lambda p, q: lax.dot_general(p, q, dims, preferred_element_type=F32)
    return dg(a[0], b) + dg(a[1], b)


def _sigmoid(x):
    return 1.0 / (1.0 + jnp.exp(-x))


def _softplus(x):
    return jnp.maximum(x, 0.0) + jnp.log(1.0 + jnp.exp(-jnp.abs(x)))


def _gelu(x):
    return 0.5 * x * (1.0 + jnp.tanh(math.sqrt(2.0 / math.pi) * (x + 0.044715 * (x * x * x))))


def _full(shape):
    n = len(shape)
    return pl.BlockSpec(shape, lambda *_: (0,) * n)


def _inproj_body(x_ref, gain_ref, w_ref, bg_ref, mu_ref, pr_ref, ps_ref, g_ref, prev_ref):
    @pl.when(pl.program_id(1) == 0)
    def _():
        prev_ref[...] = jnp.zeros_like(prev_ref)

    x = x_ref[...]
    xn = x * lax.rsqrt(jnp.mean(x * x, axis=-1, keepdims=True) + EPS) * gain_ref[...]
    xb = xn.astype(BF16)
    p = _dot(xb, w_ref[:, :N_RWKV_COLS])
    tt = p.shape[0]
    row = lax.broadcasted_iota(I32, p.shape, 0)
    prev = jnp.broadcast_to(prev_ref[0:1, :], p.shape)
    shifted = jnp.where(row == 0, prev, pltpu.roll(p, 1, 0))
    prev_ref[0:1, :] = p[tt - 1:tt, :]
    pr_ref[...] = p + (shifted - p) * mu_ref[...]
    ps_ref[...] = _dot(xb, w_ref[:, N_RWKV_COLS:N_RWKV_COLS + S5W])
    g_ref[...] = _sigmoid(_dot(xb, w_ref[:, N_RWKV_COLS + S5W:]) + bg_ref[...])


def _inproj(x2, gain, w_in_bf, b_gate, mu, nb, tiles, tt):
    T = x2.shape[0]
    ncols = w_in_bf.shape[1]
    row = lambda b, t: (b * tiles + t, 0)
    return pl.pallas_call(
        _inproj_body,
        grid=(nb, tiles),
        in_specs=[pl.BlockSpec((tt, D_MODEL), row), _full((1, D_MODEL)), _full((D_MODEL, ncols)),
                  _full((1, 2 * D_MODEL)), _full((1, N_RWKV_COLS))],
        out_specs=[pl.BlockSpec((tt, N_RWKV_COLS), row), pl.BlockSpec((tt, S5W), row),
                   pl.BlockSpec((tt, 2 * D_MODEL), row)],
        out_shape=[jax.ShapeDtypeStruct((T, N_RWKV_COLS), F32), jax.ShapeDtypeStruct((T, S5W), F32),
                   jax.ShapeDtypeStruct((T, 2 * D_MODEL), F32)],
        scratch_shapes=[pltpu.VMEM((8, N_RWKV_COLS), F32)],
        compiler_params=pltpu.CompilerParams(dimension_semantics=("arbitrary", "arbitrary"),
                                             vmem_limit_bytes=VMEM_LIMIT),
        name="inproj",
    )(x2, gain, w_in_bf, b_gate, mu)


def _rwkv_body(pr_ref, wup_ref, w0_ref, aup_ref, a0_ref, gup_ref, kk_ref, ka_ref, rk_ref, lnw_ref, lnb_ref,
               bd_ref, wo_ref, ya_ref, st_ref, y_scr, *, nc):
    @pl.when(pl.program_id(1) == 0)
    def _():
        st_ref[...] = jnp.zeros_like(st_ref)

    L = CHUNK
    r = pr_ref[:, 0:RW]
    k = pr_ref[:, RW:2 * RW]
    v = pr_ref[:, 2 * RW:3 * RW]
    x128 = pr_ref[:, 3 * RW:3 * RW + 128]
    xg = pr_ref[:, 3 * RW + 128:3 * RW + 256]
    bd = bd_ref[...]
    seg = lambda z: _dot2(_split(z), bd)

    w_log = -_softplus(-(w0_ref[...] + _dot(jnp.tanh(x128), wup_ref[...], HI))) - 0.5
    logw = -jnp.exp(w_log)
    a_lr = _sigmoid(a0_ref[...] + _dot(x128, aup_ref[...], HI))
    g = _dot(_sigmoid(xg), gup_ref[...], HI)
    kk = k * kk_ref[...]
    k2 = k * (1.0 + (a_lr - 1.0) * ka_ref[...])
    kk = kk / jnp.maximum(jnp.sqrt(seg(kk * kk)), 1e-12)
    b = kk * a_lr
    am = -kk

    ri = lax.broadcasted_iota(I32, (L, L), 0)
    ci = lax.broadcasted_iota(I32, (L, L), 1)
    strict = ri > ci
    incl = ri >= ci
    eye = (ri == ci).astype(F32)
    tri = incl.astype(BF16)
    quad = []
    bit = 0
    while (1 << bit) < L:
        s = 1 << bit
        quad.append((((ri >> (bit + 1)) == (ci >> (bit + 1))) & ((ri & s) != 0) & ((ci & s) == 0)).astype(F32))
        bit += 1
    sl = [slice(h * HD, (h + 1) * HD) for h in range(NH)]
    hr = range(NH)

    m_all, cc_all, r2_all, y0_all = [], [], [], []
    for c_i in range(nc):
        rows = slice(c_i * L, (c_i + 1) * L)
        lw = logw[rows]
        lw_h = lw.astype(BF16)
        lw_r = lw - lw_h.astype(F32)
        lw_m = lw_r.astype(BF16)
        lw_l = (lw_r - lw_m.astype(F32)).astype(BF16)
        c = _dot(tri, lw_h) + (_dot(tri, lw_m) + _dot(tri, lw_l))
        cl = c[L - 1:L, :]
        e_c = jnp.exp(c)
        e_nc = jnp.exp(-c)
        e_cp = jnp.exp(c - lw)
        e_rem = jnp.exp(cl - c)
        e_last = jnp.exp(cl)
        at_all = am[rows] * e_cp
        rt_all = r[rows] * e_c
        bt_all = b[rows] * e_nc
        kt_all = k2[rows] * e_nc
        bh_all = b[rows] * e_rem
        kh_all = k2[rows] * e_rem
        v_all = v[rows]
        at = [at_all[:, s] for s in sl]
        rt = [rt_all[:, s] for s in sl]
        q = [_dot3(_split(jnp.concatenate([at[h], rt[h]], axis=0)),
                   _split(jnp.concatenate([bt_all[:, sl[h]], kt_all[:, sl[h]]], axis=0)), _NT) for h in hr]
        n_ab = [jnp.where(strict, q[h][:L, :L], 0.0) for h in hr]
        a_kr = [_split(jnp.concatenate([jnp.where(strict, q[h][:L, L:], 0.0),
                                        jnp.where(incl, q[h][L:, L:], 0.0)], axis=0)) for h in hr]
        a_rb = [_split(jnp.where(incl, q[h][L:, :L], 0.0)) for h in hr]
        dinv = [eye + n_ab[h] * quad[0] for h in hr]
        for qm in quad[1:]:
            ds = [_split(dinv[h]) for h in hr]
            t1 = [_dot3(_split(n_ab[h] * qm), ds[h]) for h in hr]
            dinv = [dinv[h] + _dot3(ds[h], _split(t1[h])) for h in hr]
        ds = [_split(dinv[h]) for h in hr]
        vs = [_split(v_all[:, s]) for s in sl]
        av = [_dot3(a_kr[h], vs[h]) for h in hr]
        at2 = [_dot3(ds[h], _split(at[h])) for h in hr]
        u0 = [_dot3(ds[h], _split(av[h][:L])) for h in hr]
        at2s = [_split(at2[h]) for h in hr]
        u0s = [_split(u0[h]) for h in hr]
        bhs = [_split(bh_all[:, s]) for s in sl]
        khs = [_split(kh_all[:, s]) for s in sl]
        m_all.append([eye * e_last[:, sl[h]] + _dot3(_split(at2[h].T), bhs[h]) for h in hr])
        cc_all.append([_dot3(_split(u0[h].T), bhs[h]) + _dot3(_split(v_all[:, sl[h]].T), khs[h]) for h in hr])
        r2_all.append([_split(rt[h] + _dot3(a_rb[h], at2s[h])) for h in hr])
        y0_all.append([_dot3(a_rb[h], u0s[h]) + av[h][L:] for h in hr])

    st = [st_ref[h] for h in hr]
    for c_i in range(nc):
        rows = slice(c_i * L, (c_i + 1) * L)
        ss = [_split(st[h]) for h in hr]
        for h in hr:
            y_scr[rows, sl[h]] = y0_all[c_i][h] + _dot3(r2_all[c_i][h], ss[h], _NT)
        st = [_dot3(ss[h], _split(m_all[c_i][h])) + cc_all[c_i][h] for h in hr]
    for h in hr:
        st_ref[h] = st[h]

    y = y_scr[...]
    mean = seg(y) * (1.0 / HD)
    yc = y - mean
    var = seg(yc * yc) * (1.0 / HD)
    yn = yc * lax.rsqrt(var + GN_EPS) * lnw_ref[...] + lnb_ref[...]
    bonus = seg(r * k2 * rk_ref[...]) * v
    out = (yn + bonus) * g
    ya_ref[...] = _dot(out.astype(BF16), wo_ref[...])


def _rwkv(pr, wup, w0, aup, a0, gup, k_k, k_a, r_k, ln_w, ln_b, bd, wo_bf, nb, tiles, tc):
    T = pr.shape[0]
    row = lambda b, t: (b * tiles + t, 0)
    return pl.pallas_call(
        functools.partial(_rwkv_body, nc=tc // CHUNK),
        grid=(nb, tiles),
        in_specs=[pl.BlockSpec((tc, N_RWKV_COLS), row), _full((128, RW)), _full((1, RW)), _full((128, RW)),
                  _full((1, RW)), _full((128, RW)), _full((1, RW)), _full((1, RW)), _full((1, RW)),
                  _full((1, RW)), _full((1, RW)), _full((RW, RW)), _full((RW, D_MODEL))],
        out_specs=pl.BlockSpec((tc, D_MODEL), row),
        out_shape=jax.ShapeDtypeStruct((T, D_MODEL), F32),
        scratch_shapes=[pltpu.VMEM((NH, HD, HD), F32), pltpu.VMEM((tc, RW), F32)],
        compiler_params=pltpu.CompilerParams(dimension_semantics=("arbitrary", "arbitrary"),
                                             vmem_limit_bytes=VMEM_LIMIT),
        name="rwkv",
    )(pr, wup, w0, aup, a0, gup, k_k, k_a, r_k, ln_w, ln_b, bd, wo_bf)


def _s5_prep_body(ldt_r, are_r, aim_r, ldt_c, are_c, aim_c, bre_ref, bim_ref, lpr_ref, lpi_ref, bbr_ref, bbi_ref, *, tc):
    dt = jnp.exp(ldt_r[...])
    mag = jnp.exp(are_r[...] * dt)
    ang = aim_r[...] * dt
    lpr_ref[0:1, :] = mag * jnp.cos(ang)
    lpi_ref[0:1, :] = mag * jnp.sin(ang)
    n = 1
    while n < tc:
        pr = lpr_ref[n - 1:n, :]
        pi = lpi_ref[n - 1:n, :]
        qr = lpr_ref[0:n, :]
        qi = lpi_ref[0:n, :]
        lpr_ref[n:2 * n, :] = qr * pr - qi * pi
        lpi_ref[n:2 * n, :] = qr * pi + qi * pr
        n *= 2
    dtc = jnp.exp(ldt_c[...])
    are = are_c[...]
    aim = aim_c[...]
    magc = jnp.exp(are * dtc)
    angc = aim * dtc
    lre = magc * jnp.cos(angc)
    lim = magc * jnp.sin(angc)
    den = are * are + aim * aim
    nre = lre - 1.0
    fre = (nre * are + lim * aim) / den
    fim = (lim * are - nre * aim) / den
    bre = bre_ref[...]
    bim = bim_ref[...]
    bbr_ref[...] = fre * bre - fim * bim
    bbi_ref[...] = fre * bim + fim * bre


def _s5_prep(log_dt, a_re, a_im, b_re, b_im, tc):
    ng, ns = a_re.shape
    hh = b_re.shape[-1]
    n = ng * ns
    ldt = jnp.broadcast_to(log_dt[:, None], (ng, ns))
    row = lambda z: z.reshape(1, n)
    col = lambda z: z.reshape(n, 1)
    return pl.pallas_call(
        functools.partial(_s5_prep_body, tc=tc),
        out_shape=[jax.ShapeDtypeStruct((tc, n), F32), jax.ShapeDtypeStruct((tc, n), F32),
                   jax.ShapeDtypeStruct((n, hh), F32), jax.ShapeDtypeStruct((n, hh), F32)],
        name="s5_prep",
    )(row(ldt), row(a_re), row(a_im), col(ldt), col(a_re), col(a_im), b_re.reshape(n, hh), b_im.reshape(n, hh))


def _s5_body(u_ref, bw_ref, cw_ref, d_ref, lpr_ref, lpi_ref, wglu_ref, yb_ref, st_ref):
    @pl.when(pl.program_id(1) == 0)
    def _():
        st_ref[...] = jnp.zeros_like(st_ref)

    u = u_ref[...]
    tc = u.shape[0]
    nblk = bw_ref.shape[0]
    sb = S5_STATES // nblk
    ub = S5W // nblk
    bus = [_dot(u[:, j * ub:(j + 1) * ub], bw_ref[j], HI) for j in range(nblk)]
    xr = jnp.concatenate([bu[:, :sb] for bu in bus], axis=1)
    xi = jnp.concatenate([bu[:, sb:] for bu in bus], axis=1)
    row = lax.broadcasted_iota(I32, (tc, 1), 0)
    d = 1
    while d < tc:
        lr = lpr_ref[d - 1:d, :]
        li = lpi_ref[d - 1:d, :]
        keep = row >= d
        sr = jnp.where(keep, pltpu.roll(xr, d, 0), 0.0)
        si = jnp.where(keep, pltpu.roll(xi, d, 0), 0.0)
        xr, xi = xr + lr * sr - li * si, xi + lr * si + li * sr
        d *= 2
    pr = st_ref[0:1, :]
    pi = st_ref[1:2, :]
    lr = lpr_ref[...]
    li = lpi_ref[...]
    xr, xi = xr + lr * pr - li * pi, xi + lr * pi + li * pr
    st_ref[0:1, :] = xr[tc - 1:tc, :]
    st_ref[1:2, :] = xi[tc - 1:tc, :]
    ys = [_dot(jnp.concatenate([xr[:, j * sb:(j + 1) * sb], xi[:, j * sb:(j + 1) * sb]], axis=1), cw_ref[j], HI)
          for j in range(nblk)]
    y = jnp.concatenate(ys, axis=1) + d_ref[...] * u
    z = _dot(_gelu(y).astype(BF16), wglu_ref[...])
    yb_ref[...] = z[:, :D_MODEL] * _sigmoid(z[:, D_MODEL:])


def _s5(ps, bw, cw, dd, lpr, lpi, wglu_bf, nb, tiles, tc):
    T = ps.shape[0]
    row = lambda b, t: (b * tiles + t, 0)
    return pl.pallas_call(
        _s5_body,
        grid=(nb, tiles),
        in_specs=[pl.BlockSpec((tc, S5W), row), _full(bw.shape), _full(cw.shape), _full((1, S5W)),
                  _full((tc, S5_STATES)), _full((tc, S5_STATES)), _full((S5W, 2 * D_MODEL))],
        out_specs=pl.BlockSpec((tc, D_MODEL), row),
        out_shape=jax.ShapeDtypeStruct((T, D_MODEL), F32),
        scratch_shapes=[pltpu.VMEM((8, S5_STATES), F32)],
        compiler_params=pltpu.CompilerParams(dimension_semantics=("arbitrary", "arbitrary"),
                                             vmem_limit_bytes=VMEM_LIMIT),
        name="s5",
    )(ps, bw, cw, dd, lpr, lpi, wglu_bf)


def _stage2_layout():
    blocks = [("row", 0, 16), ("row", 1, 8), ("row", 2, 8), ("row", 3, 8),
              ("col", 0, 16), ("col", 1, 8), ("col", 2, 8)]
    flat = []
    for kind, a, n in blocks:
        for m in range(n):
            i, j = (a, m) if kind == "row" else (m, a)
            ok = (i + 1) * (j + 1) <= PEER_TOPK and (kind == "row" or i >= 4)
            flat.append(i * PEER_TOPK + j if ok else -1)
    return blocks, flat


def _topk_rows(vals, key, n_out, big):
    out_v, out_k = [], []
    for _ in range(n_out):
        m = jnp.max(vals, axis=0, keepdims=True)
        sel = jnp.min(jnp.where(vals == m, key, big), axis=0, keepdims=True)
        out_v.append(m)
        out_k.append(sel)
        vals = jnp.where(key == sel, -jnp.inf, vals)
    return out_v, out_k


def _mix_body(x_ref, ya_ref, yb_ref, g_ref, wout_ref, nf_ref, wq_ref, keys_ref, flat_ref,
              h_ref, xn_ref, ids_ref, gate_ref, sv_scr, si_scr):
    g = g_ref[...]
    mixed = g[:, :D_MODEL] * ya_ref[...] + g[:, D_MODEL:] * yb_ref[...]
    h = x_ref[...] + _dot(mixed.astype(BF16), wout_ref[...])
    h_ref[...] = h
    xn = h * lax.rsqrt(jnp.mean(h * h, axis=-1, keepdims=True) + EPS) * nf_ref[...]
    xn_ref[...] = xn
    q = _dot(xn, wq_ref[...], HI)
    tt = q.shape[0]
    key_iota = lax.broadcasted_iota(I32, (PEER_KEYS, tt), 0)
    blocks, _ = _stage2_layout()
    flat = flat_ref[...]
    flat_b = jnp.broadcast_to(flat, (flat.shape[0], tt))
    for hd in range(PEER_HEADS):
        for half in range(2):
            col = (hd * 2 + half) * PEER_HALF
            s_t = _dot_nt(keys_ref[hd * 2 + half], q[:, col:col + PEER_HALF])
            vs, ks = _topk_rows(s_t, key_iota, PEER_TOPK, PEER_KEYS)
            for i in range(PEER_TOPK):
                sv_scr[half, i:i + 1, :] = vs[i]
                si_scr[half, i:i + 1, :] = ks[i]
        sv1, sv2 = sv_scr[0], sv_scr[1]
        si1, si2 = si_scr[0], si_scr[1]
        cand, eid = [], []
        for kind, a, n in blocks:
            if kind == "row":
                cand.append(sv1[a:a + 1, :] + sv2[0:n, :])
                eid.append(si1[a:a + 1, :] * PEER_KEYS + si2[0:n, :])
            else:
                cand.append(sv1[0:n, :] + sv2[a:a + 1, :])
                eid.append(si1[0:n, :] * PEER_KEYS + si2[a:a + 1, :])
        cand = jnp.where(flat_b >= 0, jnp.concatenate(cand, axis=0), -jnp.inf)
        eid = jnp.concatenate(eid, axis=0)
        cv, cf = _topk_rows(cand, flat_b, PEER_TOPK, PEER_TOPK * PEER_TOPK)
        ex = [jnp.exp(c - cv[0]) for c in cv]
        tot = ex[0]
        for e in ex[1:]:
            tot = tot + e
        for i in range(PEER_TOPK):
            r = hd * PEER_TOPK + i
            ids_ref[r:r + 1, :] = jnp.sum(jnp.where(flat_b == cf[i], eid, 0), axis=0, keepdims=True)
            gate_ref[r:r + 1, :] = ex[i] / tot


def _mix(x2, ya, yb, g, wout_bf, norm_ffn, wq_bf, keys, flat, tt):
    T = x2.shape[0]
    row = lambda i: (i, 0)
    colb = lambda i: (0, i)
    nk = PEER_HEADS * PEER_TOPK
    return pl.pallas_call(
        _mix_body,
        grid=(T // tt,),
        in_specs=[pl.BlockSpec((tt, D_MODEL), row), pl.BlockSpec((tt, D_MODEL), row),
                  pl.BlockSpec((tt, D_MODEL), row), pl.BlockSpec((tt, 2 * D_MODEL), row),
                  _full((D_MODEL, D_MODEL)), _full((1, D_MODEL)), _full((D_MODEL, D_MODEL)),
                  _full(keys.shape), _full(flat.shape)],
        out_specs=[pl.BlockSpec((tt, D_MODEL), row), pl.BlockSpec((tt, D_MODEL), row),
                   pl.BlockSpec((nk, tt), colb), pl.BlockSpec((nk, tt), colb)],
        out_shape=[jax.ShapeDtypeStruct((T, D_MODEL), F32), jax.ShapeDtypeStruct((T, D_MODEL), F32),
                   jax.ShapeDtypeStruct((nk, T), I32), jax.ShapeDtypeStruct((nk, T), F32)],
        scratch_shapes=[pltpu.VMEM((2, PEER_TOPK, tt), F32), pltpu.VMEM((2, PEER_TOPK, tt), I32)],
        compiler_params=pltpu.CompilerParams(dimension_semantics=("arbitrary",),
                                             vmem_limit_bytes=VMEM_LIMIT),
        name="mix_topk",
    )(x2, ya, yb, g, wout_bf, norm_ffn, wq_bf, keys, flat)


NK = PEER_HEADS * PEER_TOPK
SLAB = D_MODEL // 128
GROUP = 8


def _table(tab):
    return tab.astype(BF16).reshape(tab.shape[0], SLAB, 128)


def _slab_diag():
    lane = lax.broadcasted_iota(I32, (SLAB, NK * SLAB), 1)
    sub = lax.broadcasted_iota(I32, (SLAB, NK * SLAB), 0)
    return sub == (lane & (SLAB - 1))


def _gather_token(ids_ref, tab_ref, t, g_scr):
    ids_t = ids_ref.at[t]
    for mi in range(NK):
        g_scr[mi * SLAB:(mi + 1) * SLAB, :] = tab_ref[ids_t[mi]]


def _peer_hidden_body(ids_ref, x_ref, gate_ref, tab_ref, pool_ref, c_ref, ga_scr, gb_scr, s_scr):
    tb = x_ref.shape[0]
    diag = _slab_diag()

    def group(gi, carry):
        base = pl.multiple_of(gi * GROUP, GROUP)
        for i in range(GROUP):
            t = base + i
            g_scr = (ga_scr, gb_scr)[i % 2]
            _gather_token(ids_ref, tab_ref, t, g_scr)
            x3 = x_ref[t]
            r = _dot3_lhs(x3, g_scr[...], _NT)
            s_scr[i:i + 1, :] = jnp.sum(jnp.where(diag, r, 0.0), axis=0, keepdims=True)
        s8 = s_scr[...]
        s_hi = s8.astype(BF16)
        s_r = s8 - s_hi.astype(F32)
        s_mid = s_r.astype(BF16)
        s_lo = (s_r - s_mid.astype(F32)).astype(BF16)
        pool = pool_ref[...]
        hid = _dot(s_hi, pool) + (_dot(s_mid, pool) + _dot(s_lo, pool))
        rows = pl.ds(base, GROUP)
        c_ref[rows, :] = gate_ref[rows, :] * _gelu(hid)
        return carry

    lax.fori_loop(0, tb // GROUP, group, 0)


def _dot3_lhs(a, b_bf, dims):
    m = a.shape[0]
    a_hi = a.astype(BF16)
    a_lo = (a - a_hi.astype(F32)).astype(BF16)
    r = lax.dot_general(jnp.concatenate([a_hi, a_lo], axis=0), b_bf, dims, preferred_element_type=F32)
    return r[:m] + r[m:]


def _peer_hidden(ids, x3, gate, tab, tb):
    T = ids.shape[0]
    row = lambda i: (i, 0)
    pool = (jnp.arange(NK * SLAB)[:, None] // SLAB == jnp.arange(NK)[None, :]).astype(BF16)
    stage = pltpu.VMEM((NK * SLAB, 128), BF16)
    return pl.pallas_call(
        _peer_hidden_body,
        grid=(T // tb,),
        in_specs=[pl.BlockSpec((tb, NK), row, memory_space=pltpu.SMEM),
                  pl.BlockSpec((tb, SLAB, 128), lambda i: (i, 0, 0)), pl.BlockSpec((tb, NK), row),
                  _full(tab.shape), _full(pool.shape)],
        out_specs=pl.BlockSpec((tb, NK), row),
        out_shape=jax.ShapeDtypeStruct((T, NK), F32),
        scratch_shapes=[stage, stage, pltpu.VMEM((GROUP, NK * SLAB), F32)],
        compiler_params=pltpu.CompilerParams(dimension_semantics=("arbitrary",),
                                             vmem_limit_bytes=VMEM_LIMIT),
        name="peer_hidden",
    )(ids, x3, gate, tab, pool)


def _peer_out_body(ids_ref, c8_ref, tab_ref, o_ref, ga_scr, gb_scr, gc_scr, gd_scr):
    tb = o_ref.shape[0]
    diag = _slab_diag()

    def gather(t, g_scr):
        _gather_token(ids_ref, tab_ref, t, g_scr)

    def combine(t, g_scr):
        c8 = jnp.broadcast_to(c8_ref[pl.ds(t, 1), :], (SLAB, NK * SLAB))
        o_ref[t] = _dot3_lhs(jnp.where(diag, c8, 0.0), g_scr[...], _NN)

    def quad(q, carry):
        t0 = 4 * q
        combine(t0, ga_scr)
        combine(t0 + 1, gb_scr)
        gather(t0 + 2, gc_scr)
        gather(t0 + 3, gd_scr)
        combine(t0 + 2, gc_scr)
        combine(t0 + 3, gd_scr)
        t4 = jnp.minimum(t0 + 4, tb - 2)
        gather(t4, ga_scr)
        gather(t4 + 1, gb_scr)
        return carry

    gather(0, ga_scr)
    gather(1, gb_scr)
    lax.fori_loop(0, tb // 4, quad, 0)


def _peer_out(ids, c, tab, tb):
    T = ids.shape[0]
    row = lambda i: (i, 0)
    c8 = jnp.repeat(c, SLAB, axis=1)
    stage = pltpu.VMEM((NK * SLAB, 128), BF16)
    return pl.pallas_call(
        _peer_out_body,
        grid=(T // tb,),
        in_specs=[pl.BlockSpec((tb, NK), row, memory_space=pltpu.SMEM),
                  pl.BlockSpec((tb, NK * SLAB), row), _full(tab.shape)],
        out_specs=pl.BlockSpec((tb, SLAB, 128), lambda i: (i, 0, 0)),
        out_shape=jax.ShapeDtypeStruct((T, SLAB, 128), F32),
        scratch_shapes=[stage] * 4,
        compiler_params=pltpu.CompilerParams(dimension_semantics=("arbitrary",),
                                             vmem_limit_bytes=VMEM_LIMIT),
        name="peer_out",
    )(ids, c8, tab)


def _final_body(h_ref, p_ref, gain_ref, o_ref):
    x = h_ref[...] + p_ref[...]
    o_ref[...] = x * lax.rsqrt(jnp.mean(x * x, axis=-1, keepdims=True) + EPS) * gain_ref[...]


def _final(h, p, gain, tt):
    T = h.shape[0]
    row = lambda i: (i, 0)
    return pl.pallas_call(
        _final_body,
        grid=(T // tt,),
        in_specs=[pl.BlockSpec((tt, D_MODEL), row), pl.BlockSpec((tt, D_MODEL), row), _full((1, D_MODEL))],
        out_specs=pl.BlockSpec((tt, D_MODEL), row),
        out_shape=jax.ShapeDtypeStruct((T, D_MODEL), F32),
        compiler_params=pltpu.CompilerParams(dimension_semantics=("arbitrary",)),
        name="final_norm",
    )(h, p, gain)


def _tile(n, pref):
    t = min(n, pref)
    assert n % t == 0, (n, t)
    return t


def kernel(x, norm_mix, w_in, b_gate, mu_rwkv, w_lora_up, w0, a_lora_up, a0, g_lora_up, k_k, k_a, r_k, ln_x_w, ln_x_b, w_o_rwkv, s5_log_dt, s5_a_re, s5_a_im, s5_b_re, s5_b_im, s5_c_re, s5_c_im, s5_d, w_glu_s5, w_out, norm_ffn, peer_wq, peer_subkeys, peer_u, peer_v, norm_final):
    B, S, D = x.shape
    assert D == D_MODEL and w_in.shape[0] == 1 and S % CHUNK == 0
    T = B * S
    x2 = x.reshape(T, D)
    r1 = lambda z: z.reshape(1, -1)

    tt = _tile(S, 256)
    pr, ps, g = _inproj(x2, r1(norm_mix[0]), w_in[0].astype(BF16), r1(b_gate[0]), r1(mu_rwkv[0]), B, S // tt, tt)

    zpad = jnp.zeros((64, RW), F32)
    wup = jnp.concatenate([w_lora_up[0], zpad], axis=0)
    aup = jnp.concatenate([zpad, a_lora_up[0]], axis=0)
    hid = jnp.arange(RW) // HD
    bd = (hid[:, None] == hid[None, :]).astype(BF16)
    tc = _tile(S, 256)
    ya = _rwkv(pr, wup, r1(w0[0]), aup, r1(a0[0]), g_lora_up[0], r1(k_k[0]), r1(k_a[0]), r1(r_k[0]),
               r1(ln_x_w[0]), r1(ln_x_b[0]), bd, w_o_rwkv[0].astype(BF16), B, S // tc, tc)

    ts = _tile(S, 256)
    lpr, lpi, bbr, bbi = _s5_prep(s5_log_dt[0], s5_a_re[0], s5_a_im[0], s5_b_re[0], s5_b_im[0], ts)
    ng, ns = s5_a_re[0].shape
    hh = s5_b_re.shape[-1]
    nblk, gb = 4, ng // 4
    eye_g = jnp.eye(gb, dtype=F32)

    def in_blocks(bb):
        return jnp.einsum("jgph,gk->jghkp", bb.reshape(nblk, gb, ns, hh), eye_g).reshape(nblk, gb * hh, gb * ns)

    def out_blocks(cc):
        return jnp.einsum("jghp,gk->jgpkh", cc.reshape(nblk, gb, hh, ns), eye_g).reshape(nblk, gb * ns, gb * hh)

    bw = jnp.concatenate([in_blocks(bbr), in_blocks(bbi)], axis=2)
    cw = jnp.concatenate([out_blocks(s5_c_re[0]), -out_blocks(s5_c_im[0])], axis=1)
    yb = _s5(ps, bw, cw, r1(s5_d[0]), lpr, lpi, w_glu_s5[0].astype(BF16), B, S // ts, ts)

    _, flat = _stage2_layout()
    flat = jnp.asarray(flat, I32).reshape(-1, 1)
    keys = peer_subkeys[0].reshape(PEER_HEADS * 2, PEER_KEYS, PEER_HALF)
    tm = _tile(T, 256)
    h, xn, ids_t, gate_t = _mix(x2, ya, yb, g, w_out[0].astype(BF16), r1(norm_ffn[0]), peer_wq[0],
                                keys, flat, tm)
    ids = ids_t.T
    gate = gate_t.T

    tb = _tile(T, 256)
    c = _peer_hidden(ids, xn.reshape(T, SLAB, 128), gate, _table(peer_u[0]), tb)
    po = _peer_out(ids, c, _table(peer_v[0]), tb).reshape(T, D)
    out = _final(h, po, r1(norm_final), _tile(T, 512))
    return out.reshape(B, S, D)
```

```python
import functools
import math

import jax
import jax.numpy as jnp
from jax import lax
from jax.experimental import pallas as pl
from jax.experimental.pallas import tpu as pltpu

F32 = jnp.float32
BF16 = jnp.bfloat16
I32 = jnp.int32
HI = lax.Precision.HIGHEST

EPS = 1e-6
GN_EPS = 64e-5
D_MODEL = 1024
RW = 512
NH = 8
HD = 64
N_RWKV_COLS = 1792
S5W = 512
S5_STATES = 2048
PEER_HEADS = 8
PEER_KEYS = 128
PEER_TOPK = 16
PEER_HALF = 64
CHUNK = 64

VMEM_LIMIT = 56 * 1024 * 1024


def _dot(a, b, prec=None):
    return jnp.dot(a, b, preferred_element_type=F32, precision=prec)


def _dot_nt(a, b, prec=HI):
    return lax.dot_general(a, b, (((1,), (1,)), ((), ())), preferred_element_type=F32, precision=prec)


_NN = (((1,), (0,)), ((), ()))
_NT = (((1,), (1,)), ((), ()))
_TN = (((0,), (0,)), ((), ()))


def _split(x):
    hi = x.astype(BF16)
    return hi, (x - hi.astype(F32)).astype(BF16)


def _dot3(a, b, dims=_NN):
    dg = lambda p, q: lax.dot_general(p, q, dims, preferred_element_type=F32)
    return dg(a[0], b[0]) + (dg(a[0], b[1]) + dg(a[1], b[0]))


def _dot2(a, b, dims=_NN):
    dg = lambda p, q: lax.dot_general(p, q, dims, preferred_element_type=F32)
    return dg(a[0], b) + dg(a[1], b)


def _sigmoid(x):
    return 1.0 / (1.0 + jnp.exp(-x))


def _softplus(x):
    return jnp.maximum(x, 0.0) + jnp.log(1.0 + jnp.exp(-jnp.abs(x)))


def _gelu(x):
    return 0.5 * x * (1.0 + jnp.tanh(math.sqrt(2.0 / math.pi) * (x + 0.044715 * (x * x * x))))


def _full(shape):
    n = len(shape)
    return pl.BlockSpec(shape, lambda *_: (0,) * n)


def _inproj_body(x_ref, gain_ref, w_ref, bg_ref, mu_ref, pr_ref, ps_ref, g_ref, prev_ref):
    @pl.when(pl.program_id(1) == 0)
    def _():
        prev_ref[...] = jnp.zeros_like(prev_ref)

    x = x_ref[...]
    xn = x * lax.rsqrt(jnp.mean(x * x, axis=-1, keepdims=True) + EPS) * gain_ref[...]
    xb = xn.astype(BF16)
    p = _dot(xb, w_ref[:, :N_RWKV_COLS])
    tt = p.shape[0]
    row = lax.broadcasted_iota(I32, p.shape, 0)
    prev = jnp.broadcast_to(prev_ref[0:1, :], p.shape)
    shifted = jnp.where(row == 0, prev, pltpu.roll(p, 1, 0))
    prev_ref[0:1, :] = p[tt - 1:tt, :]
    pr_ref[...] = p + (shifted - p) * mu_ref[...]
    ps_ref[...] = _dot(xb, w_ref[:, N_RWKV_COLS:N_RWKV_COLS + S5W])
    g_ref[...] = _sigmoid(_dot(xb, w_ref[:, N_RWKV_COLS + S5W:]) + bg_ref[...])


def _inproj(x2, gain, w_in_bf, b_gate, mu, nb, tiles, tt):
    T = x2.shape[0]
    ncols = w_in_bf.shape[1]
    row = lambda b, t: (b * tiles + t, 0)
    return pl.pallas_call(
        _inproj_body,
        grid=(nb, tiles),
        in_specs=[pl.BlockSpec((tt, D_MODEL), row), _full((1, D_MODEL)), _full((D_MODEL, ncols)),
                  _full((1, 2 * D_MODEL)), _full((1, N_RWKV_COLS))],
        out_specs=[pl.BlockSpec((tt, N_RWKV_COLS), row), pl.BlockSpec((tt, S5W), row),
                   pl.BlockSpec((tt, 2 * D_MODEL), row)],
        out_shape=[jax.ShapeDtypeStruct((T, N_RWKV_COLS), F32), jax.ShapeDtypeStruct((T, S5W), F32),
                   jax.ShapeDtypeStruct((T, 2 * D_MODEL), F32)],
        scratch_shapes=[pltpu.VMEM((8, N_RWKV_COLS), F32)],
        compiler_params=pltpu.CompilerParams(dimension_semantics=("arbitrary", "arbitrary"),
                                             vmem_limit_bytes=VMEM_LIMIT),
        name="inproj",
    )(x2, gain, w_in_bf, b_gate, mu)


def _rwkv_body(pr_ref, wup_ref, w0_ref, aup_ref, a0_ref, gup_ref, kk_ref, ka_ref, rk_ref, lnw_ref, lnb_ref,
               bd_ref, wo_ref, ya_ref, st_ref, y_scr, *, nc):
    @pl.when(pl.program_id(1) == 0)
    def _():
        st_ref[...] = jnp.zeros_like(st_ref)

    L = CHUNK
    r = pr_ref[:, 0:RW]
    k = pr_ref[:, RW:2 * RW]
    v = pr_ref[:, 2 * RW:3 * RW]
    x128 = pr_ref[:, 3 * RW:3 * RW + 128]
    xg = pr_ref[:, 3 * RW + 128:3 * RW + 256]
    bd = bd_ref[...]
    seg = lambda z: _dot2(_split(z), bd)

    w_log = -_softplus(-(w0_ref[...] + _dot(jnp.tanh(x128), wup_ref[...], HI))) - 0.5
    logw = -jnp.exp(w_log)
    a_lr = _sigmoid(a0_ref[...] + _dot(x128, aup_ref[...], HI))
    g = _dot(_sigmoid(xg), gup_ref[...], HI)
    kk = k * kk_ref[...]
    k2 = k * (1.0 + (a_lr - 1.0) * ka_ref[...])
    kk = kk / jnp.maximum(jnp.sqrt(seg(kk * kk)), 1e-12)
    b = kk * a_lr
    am = -kk

    ri = lax.broadcasted_iota(I32, (L, L), 0)
    ci = lax.broadcasted_iota(I32, (L, L), 1)
    strict = ri > ci
    incl = ri >= ci
    eye = (ri == ci).astype(F32)
    tri = incl.astype(BF16)
    quad = []
    bit = 0
    while (1 << bit) < L:
        s = 1 << bit
        quad.append((((ri >> (bit + 1)) == (ci >> (bit + 1))) & ((ri & s) != 0) & ((ci & s) == 0)).astype(F32))
        bit += 1
    sl = [slice(h * HD, (h + 1) * HD) for h in range(NH)]
    hr = range(NH)

    m_all, cc_all, r2_all, y0_all = [], [], [], []
    for c_i in range(nc):
        rows = slice(c_i * L, (c_i + 1) * L)
        lw = logw[rows]
        lw_h = lw.astype(BF16)
        lw_r = lw - lw_h.astype(F32)
        lw_m = lw_r.astype(BF16)
        lw_l = (lw_r - lw_m.astype(F32)).astype(BF16)
        c = _dot(tri, lw_h) + (_dot(tri, lw_m) + _dot(tri, lw_l))
        cl = c[L - 1:L, :]
        e_c = jnp.exp(c)
        e_nc = jnp.exp(-c)
        e_cp = jnp.exp(c - lw)
        e_rem = jnp.exp(cl - c)
        e_last = jnp.exp(cl)
        at_all = am[rows] * e_cp
        rt_all = r[rows] * e_c
        bt_all = b[rows] * e_nc
        kt_all = k2[rows] * e_nc
        bh_all = b[rows] * e_rem
        kh_all = k2[rows] * e_rem
        v_all = v[rows]
        at = [at_all[:, s] for s in sl]
        rt = [rt_all[:, s] for s in sl]
        q = [_dot3(_split(jnp.concatenate([at[h], rt[h]], axis=0)),
                   _split(jnp.concatenate([bt_all[:, sl[h]], kt_all[:, sl[h]]], axis=0)), _NT) for h in hr]
        n_ab = [jnp.where(strict, q[h][:L, :L], 0.0) for h in hr]
        a_kr = [_split(jnp.concatenate([jnp.where(strict, q[h][:L, L:], 0.0),
                                        jnp.where(incl, q[h][L:, L:], 0.0)], axis=0)) for h in hr]
        a_rb = [_split(jnp.where(incl, q[h][L:, :L], 0.0)) for h in hr]
        dinv = [eye + n_ab[h] * quad[0] for h in hr]
        for qm in quad[1:]:
            ds = [_split(dinv[h]) for h in hr]
            t1 = [_dot3(_split(n_ab[h] * qm), ds[h]) for h in hr]
            dinv = [dinv[h] + _dot3(ds[h], _split(t1[h])) for h in hr]
        ds = [_split(dinv[h]) for h in hr]
        vs = [_split(v_all[:, s]) for s in sl]
        av = [_dot3(a_kr[h], vs[h]) for h in hr]
        at2 = [_dot3(ds[h], _split(at[h])) for h in hr]
        u0 = [_dot3(ds[h], _split(av[h][:L])) for h in hr]
        at2s = [_split(at2[h]) for h in hr]
        u0s = [_split(u0[h]) for h in hr]
        bhs = [_split(bh_all[:, s]) for s in sl]
        khs = [_split(kh_all[:, s]) for s in sl]
        m_all.append([eye * e_last[:, sl[h]] + _dot3(_split(at2[h].T), bhs[h]) for h in hr])
        cc_all.append([_dot3(_split(u0[h].T), bhs[h]) + _dot3(_split(v_all[:, sl[h]].T), khs[h]) for h in hr])
        r2_all.append([_split(rt[h] + _dot3(a_rb[h], at2s[h])) for h in hr])
        y0_all.append([_dot3(a_rb[h], u0s[h]) + av[h][L:] for h in hr])

    st = [st_ref[h] for h in hr]
    for c_i in range(nc):
        rows = slice(c_i * L, (c_i + 1) * L)
        ss = [_split(st[h]) for h in hr]
        for h in hr:
            y_scr[rows, sl[h]] = y0_all[c_i][h] + _dot3(r2_all[c_i][h], ss[h], _NT)
        st = [_dot3(ss[h], _split(m_all[c_i][h])) + cc_all[c_i][h] for h in hr]
    for h in hr:
        st_ref[h] = st[h]

    y = y_scr[...]
    mean = seg(y) * (1.0 / HD)
    yc = y - mean
    var = seg(yc * yc) * (1.0 / HD)
    yn = yc * lax.rsqrt(var + GN_EPS) * lnw_ref[...] + lnb_ref[...]
    bonus = seg(r * k2 * rk_ref[...]) * v
    out = (yn + bonus) * g
    ya_ref[...] = _dot(out.astype(BF16), wo_ref[...])


def _rwkv(pr, wup, w0, aup, a0, gup, k_k, k_a, r_k, ln_w, ln_b, bd, wo_bf, nb, tiles, tc):
    T = pr.shape[0]
    row = lambda b, t: (b * tiles + t, 0)
    return pl.pallas_call(
        functools.partial(_rwkv_body, nc=tc // CHUNK),
        grid=(nb, tiles),
        in_specs=[pl.BlockSpec((tc, N_RWKV_COLS), row), _full((128, RW)), _full((1, RW)), _full((128, RW)),
                  _full((1, RW)), _full((128, RW)), _full((1, RW)), _full((1, RW)), _full((1, RW)),
                  _full((1, RW)), _full((1, RW)), _full((RW, RW)), _full((RW, D_MODEL))],
        out_specs=pl.BlockSpec((tc, D_MODEL), row),
        out_shape=jax.ShapeDtypeStruct((T, D_MODEL), F32),
        scratch_shapes=[pltpu.VMEM((NH, HD, HD), F32), pltpu.VMEM((tc, RW), F32)],
        compiler_params=pltpu.CompilerParams(dimension_semantics=("arbitrary", "arbitrary"),
                                             vmem_limit_bytes=VMEM_LIMIT),
        name="rwkv",
    )(pr, wup, w0, aup, a0, gup, k_k, k_a, r_k, ln_w, ln_b, bd, wo_bf)


def _s5_prep_body(ldt_r, are_r, aim_r, ldt_c, are_c, aim_c, bre_ref, bim_ref, lpr_ref, lpi_ref, bbr_ref, bbi_ref, *, tc):
    dt = jnp.exp(ldt_r[...])
    mag = jnp.exp(are_r[...] * dt)
    ang = aim_r[...] * dt
    lpr_ref[0:1, :] = mag * jnp.cos(ang)
    lpi_ref[0:1, :] = mag * jnp.sin(ang)
    n = 1
    while n < tc:
        pr = lpr_ref[n - 1:n, :]
        pi = lpi_ref[n - 1:n, :]
        qr = lpr_ref[0:n, :]
        qi = lpi_ref[0:n, :]
        lpr_ref[n:2 * n, :] = qr * pr - qi * pi
        lpi_ref[n:2 * n, :] = qr * pi + qi * pr
        n *= 2
    dtc = jnp.exp(ldt_c[...])
    are = are_c[...]
    aim = aim_c[...]
    magc = jnp.exp(are * dtc)
    angc = aim * dtc
    lre = magc * jnp.cos(angc)
    lim = magc * jnp.sin(angc)
    den = are * are + aim * aim
    nre = lre - 1.0
    fre = (nre * are + lim * aim) / den
    fim = (lim * are - nre * aim) / den
    bre = bre_ref[...]
    bim = bim_ref[...]
    bbr_ref[...] = fre * bre - fim * bim
    bbi_ref[...] = fre * bim + fim * bre


def _s5_prep(log_dt, a_re, a_im, b_re, b_im, tc):
    ng, ns = a_re.shape
    hh = b_re.shape[-1]
    n = ng * ns
    ldt = jnp.broadcast_to(log_dt[:, None], (ng, ns))
    row = lambda z: z.reshape(1, n)
    col = lambda z: z.reshape(n, 1)
    return pl.pallas_call(
        functools.partial(_s5_prep_body, tc=tc),
        out_shape=[jax.ShapeDtypeStruct((tc, n), F32), jax.ShapeDtypeStruct((tc, n), F32),
                   jax.ShapeDtypeStruct((n, hh), F32), jax.ShapeDtypeStruct((n, hh), F32)],
        name="s5_prep",
    )(row(ldt), row(a_re), row(a_im), col(ldt), col(a_re), col(a_im), b_re.reshape(n, hh), b_im.reshape(n, hh))


def _s5_body(u_ref, bw_ref, cw_ref, d_ref, lpr_ref, lpi_ref, wglu_ref, yb_ref, st_ref):
    @pl.when(pl.program_id(1) == 0)
    def _():
        st_ref[...] = jnp.zeros_like(st_ref)

    u = u_ref[...]
    tc = u.shape[0]
    nblk = bw_ref.shape[1]
    sb = S5_STATES // nblk
    ub = S5W // nblk
    bus = [_dot3(_split(u[:, j * ub:(j + 1) * ub]), (bw_ref[0, j], bw_ref[1, j])) for j in range(nblk)]
    xr = jnp.concatenate([bu[:, :sb] for bu in bus], axis=1)
    xi = jnp.concatenate([bu[:, sb:] for bu in bus], axis=1)
    row = lax.broadcasted_iota(I32, (tc, 1), 0)
    d = 1
    while d < tc:
        lr = lpr_ref[d - 1:d, :]
        li = lpi_ref[d - 1:d, :]
        keep = row >= d
        sr = jnp.where(keep, pltpu.roll(xr, d, 0), 0.0)
        si = jnp.where(keep, pltpu.roll(xi, d, 0), 0.0)
        xr, xi = xr + lr * sr - li * si, xi + lr * si + li * sr
        d *= 2
    pr = st_ref[0:1, :]
    pi = st_ref[1:2, :]
    lr = lpr_ref[...]
    li = lpi_ref[...]
    xr, xi = xr + lr * pr - li * pi, xi + lr * pi + li * pr
    st_ref[0:1, :] = xr[tc - 1:tc, :]
    st_ref[1:2, :] = xi[tc - 1:tc, :]
    ys = [_dot3(_split(jnp.concatenate([xr[:, j * sb:(j + 1) * sb], xi[:, j * sb:(j + 1) * sb]], axis=1)),
                (cw_ref[0, j], cw_ref[1, j])) for j in range(nblk)]
    y = jnp.concatenate(ys, axis=1) + d_ref[...] * u
    z = _dot(_gelu(y).astype(BF16), wglu_ref[...])
    yb_ref[...] = z[:, :D_MODEL] * _sigmoid(z[:, D_MODEL:])


def _s5(ps, bw, cw, dd, lpr, lpi, wglu_bf, nb, tiles, tc):
    T = ps.shape[0]
    row = lambda b, t: (b * tiles + t, 0)
    return pl.pallas_call(
        _s5_body,
        grid=(nb, tiles),
        in_specs=[pl.BlockSpec((tc, S5W), row), _full(bw.shape), _full(cw.shape), _full((1, S5W)),
                  _full((tc, S5_STATES)), _full((tc, S5_STATES)), _full((S5W, 2 * D_MODEL))],
        out_specs=pl.BlockSpec((tc, D_MODEL), row),
        out_shape=jax.ShapeDtypeStruct((T, D_MODEL), F32),
        scratch_shapes=[pltpu.VMEM((8, S5_STATES), F32)],
        compiler_params=pltpu.CompilerParams(dimension_semantics=("arbitrary", "arbitrary"),
                                             vmem_limit_bytes=VMEM_LIMIT),
        name="s5",
    )(ps, bw, cw, dd, lpr, lpi, wglu_bf)


def _stage2_layout():
    blocks = [("row", 0, 16), ("row", 1, 8), ("row", 2, 8), ("row", 3, 8),
              ("col", 0, 16), ("col", 1, 8), ("col", 2, 8)]
    flat = []
    for kind, a, n in blocks:
        for m in range(n):
            i, j = (a, m) if kind == "row" else (m, a)
            ok = (i + 1) * (j + 1) <= PEER_TOPK and (kind == "row" or i >= 4)
            flat.append(i * PEER_TOPK + j if ok else -1)
    return blocks, flat


def _topk_rows(vals, key, n_out, big):
    out_v, out_k = [], []
    for _ in range(n_out):
        m = jnp.max(vals, axis=0, keepdims=True)
        sel = jnp.min(jnp.where(vals == m, key, big), axis=0, keepdims=True)
        out_v.append(m)
        out_k.append(sel)
        vals = jnp.where(key == sel, -jnp.inf, vals)
    return out_v, out_k


def _mix_body(x_ref, ya_ref, yb_ref, g_ref, wout_ref, nf_ref, wq_ref, keys_ref, flat_ref,
              h_ref, xn_ref, ids_ref, gate_ref, sv_scr, si_scr):
    g = g_ref[...]
    mixed = g[:, :D_MODEL] * ya_ref[...] + g[:, D_MODEL:] * yb_ref[...]
    h = x_ref[...] + _dot(mixed.astype(BF16), wout_ref[...])
    h_ref[...] = h
    xn = h * lax.rsqrt(jnp.mean(h * h, axis=-1, keepdims=True) + EPS) * nf_ref[...]
    xn_ref[...] = xn
    q = _dot(xn, wq_ref[...], HI)
    tt = q.shape[0]
    key_iota = lax.broadcasted_iota(I32, (PEER_KEYS, tt), 0).astype(F32)
    blocks, _ = _stage2_layout()
    flat = flat_ref[...]
    flat_b = jnp.broadcast_to(flat, (flat.shape[0], tt)).astype(F32)
    for hd in range(PEER_HEADS):
        for half in range(2):
            col = (hd * 2 + half) * PEER_HALF
            s_t = _dot_nt(keys_ref[hd * 2 + half], q[:, col:col + PEER_HALF])
            vs, ks = _topk_rows(s_t, key_iota, PEER_TOPK, float(PEER_KEYS))
            for i in range(PEER_TOPK):
                sv_scr[half, i:i + 1, :] = vs[i]
                si_scr[half, i:i + 1, :] = ks[i].astype(I32)
        sv1, sv2 = sv_scr[0], sv_scr[1]
        si1, si2 = si_scr[0], si_scr[1]
        cand, eid = [], []
        for kind, a, n in blocks:
            if kind == "row":
                cand.append(sv1[a:a + 1, :] + sv2[0:n, :])
                eid.append(si1[a:a + 1, :] * PEER_KEYS + si2[0:n, :])
            else:
                cand.append(sv1[0:n, :] + sv2[a:a + 1, :])
                eid.append(si1[0:n, :] * PEER_KEYS + si2[a:a + 1, :])
        cand = jnp.where(flat_b >= 0, jnp.concatenate(cand, axis=0), -jnp.inf)
        eid = jnp.concatenate(eid, axis=0)
        cv, cf = _topk_rows(cand, flat_b, PEER_TOPK, float(PEER_TOPK * PEER_TOPK))
        ex = [jnp.exp(c - cv[0]) for c in cv]
        tot = ex[0]
        for e in ex[1:]:
            tot = tot + e
        for i in range(PEER_TOPK):
            r = hd * PEER_TOPK + i
            ids_ref[r:r + 1, :] = jnp.sum(jnp.where(flat_b == cf[i], eid, 0), axis=0, keepdims=True)
            gate_ref[r:r + 1, :] = ex[i] / tot


def _mix(x2, ya, yb, g, wout_bf, norm_ffn, wq_bf, keys, flat, tt):
    T = x2.shape[0]
    row = lambda i: (i, 0)
    colb = lambda i: (0, i)
    nk = PEER_HEADS * PEER_TOPK
    return pl.pallas_call(
        _mix_body,
        grid=(T // tt,),
        in_specs=[pl.BlockSpec((tt, D_MODEL), row), pl.BlockSpec((tt, D_MODEL), row),
                  pl.BlockSpec((tt, D_MODEL), row), pl.BlockSpec((tt, 2 * D_MODEL), row),
                  _full((D_MODEL, D_MODEL)), _full((1, D_MODEL)), _full((D_MODEL, D_MODEL)),
                  _full(keys.shape), _full(flat.shape)],
        out_specs=[pl.BlockSpec((tt, D_MODEL), row), pl.BlockSpec((tt, D_MODEL), row),
                   pl.BlockSpec((nk, tt), colb), pl.BlockSpec((nk, tt), colb)],
        out_shape=[jax.ShapeDtypeStruct((T, D_MODEL), F32), jax.ShapeDtypeStruct((T, D_MODEL), F32),
                   jax.ShapeDtypeStruct((nk, T), I32), jax.ShapeDtypeStruct((nk, T), F32)],
        scratch_shapes=[pltpu.VMEM((2, PEER_TOPK, tt), F32), pltpu.VMEM((2, PEER_TOPK, tt), I32)],
        compiler_params=pltpu.CompilerParams(dimension_semantics=("arbitrary",),
                                             vmem_limit_bytes=VMEM_LIMIT),
        name="mix_topk",
    )(x2, ya, yb, g, wout_bf, norm_ffn, wq_bf, keys, flat)


NK = PEER_HEADS * PEER_TOPK
SLAB = D_MODEL // 128
GROUP = 8


def _table(tab):
    return tab.astype(BF16).reshape(tab.shape[0], SLAB, 128)


def _slab_diag():
    lane = lax.broadcasted_iota(I32, (SLAB, NK * SLAB), 1)
    sub = lax.broadcasted_iota(I32, (SLAB, NK * SLAB), 0)
    return sub == (lane & (SLAB - 1))


def _gather_token(ids_ref, tab_ref, t, g_scr):
    ids_t = ids_ref.at[t]
    for mi in range(NK):
        g_scr[mi * SLAB:(mi + 1) * SLAB, :] = tab_ref[ids_t[mi]]


def _peer_hidden_body(ids_ref, x_ref, gate_ref, tab_ref, pool_ref, c_ref, ga_scr, gb_scr, s_scr):
    tb = x_ref.shape[0]
    diag = _slab_diag()

    def group(gi, carry):
        base = pl.multiple_of(gi * GROUP, GROUP)
        for i in range(GROUP):
            t = base + i
            g_scr = (ga_scr, gb_scr)[i % 2]
            _gather_token(ids_ref, tab_ref, t, g_scr)
            x3 = x_ref[t]
            r = _dot3_lhs(x3, g_scr[...], _NT)
            s_scr[i:i + 1, :] = jnp.sum(jnp.where(diag, r, 0.0), axis=0, keepdims=True)
        s8 = s_scr[...]
        s_hi = s8.astype(BF16)
        s_r = s8 - s_hi.astype(F32)
        s_mid = s_r.astype(BF16)
        s_lo = (s_r - s_mid.astype(F32)).astype(BF16)
        pool = pool_ref[...]
        hid = _dot(s_hi, pool) + (_dot(s_mid, pool) + _dot(s_lo, pool))
        rows = pl.ds(base, GROUP)
        c_ref[rows, :] = gate_ref[rows, :] * _gelu(hid)
        return carry

    lax.fori_loop(0, tb // GROUP, group, 0)


def _dot3_lhs(a, b_bf, dims):
    m = a.shape[0]
    a_hi = a.astype(BF16)
    a_lo = (a - a_hi.astype(F32)).astype(BF16)
    r = lax.dot_general(jnp.concatenate([a_hi, a_lo], axis=0), b_bf, dims, preferred_element_type=F32)
    return r[:m] + r[m:]


def _peer_hidden(ids, x3, gate, tab, tb):
    T = ids.shape[0]
    row = lambda i: (i, 0)
    pool = (jnp.arange(NK * SLAB)[:, None] // SLAB == jnp.arange(NK)[None, :]).astype(BF16)
    stage = pltpu.VMEM((NK * SLAB, 128), BF16)
    return pl.pallas_call(
        _peer_hidden_body,
        grid=(T // tb,),
        in_specs=[pl.BlockSpec((tb, NK), row, memory_space=pltpu.SMEM),
                  pl.BlockSpec((tb, SLAB, 128), lambda i: (i, 0, 0)), pl.BlockSpec((tb, NK), row),
                  _full(tab.shape), _full(pool.shape)],
        out_specs=pl.BlockSpec((tb, NK), row),
        out_shape=jax.ShapeDtypeStruct((T, NK), F32),
        scratch_shapes=[stage, stage, pltpu.VMEM((GROUP, NK * SLAB), F32)],
        compiler_params=pltpu.CompilerParams(dimension_semantics=("arbitrary",),
                                             vmem_limit_bytes=VMEM_LIMIT),
        name="peer_hidden",
    )(ids, x3, gate, tab, pool)


def _peer_out_body(ids_ref, c8_ref, tab_ref, o_ref, ga_scr, gb_scr, gc_scr, gd_scr):
    tb = o_ref.shape[0]
    diag = _slab_diag()

    def gather(t, g_scr):
        _gather_token(ids_ref, tab_ref, t, g_scr)

    def combine(t, g_scr):
        c8 = jnp.broadcast_to(c8_ref[pl.ds(t, 1), :], (SLAB, NK * SLAB))
        o_ref[t] = _dot3_lhs(jnp.where(diag, c8, 0.0), g_scr[...], _NN)

    def quad(q, carry):
        t0 = 4 * q
        combine(t0, ga_scr)
        combine(t0 + 1, gb_scr)
        gather(t0 + 2, gc_scr)
        gather(t0 + 3, gd_scr)
        combine(t0 + 2, gc_scr)
        combine(t0 + 3, gd_scr)
        t4 = jnp.minimum(t0 + 4, tb - 2)
        gather(t4, ga_scr)
        gather(t4 + 1, gb_scr)
        return carry

    gather(0, ga_scr)
    gather(1, gb_scr)
    lax.fori_loop(0, tb // 4, quad, 0)


def _peer_out(ids, c, tab, tb):
    T = ids.shape[0]
    row = lambda i: (i, 0)
    c8 = jnp.repeat(c, SLAB, axis=1)
    stage = pltpu.VMEM((NK * SLAB, 128), BF16)
    return pl.pallas_call(
        _peer_out_body,
        grid=(T // tb,),
        in_specs=[pl.BlockSpec((tb, NK), row, memory_space=pltpu.SMEM),
                  pl.BlockSpec((tb, NK * SLAB), row), _full(tab.shape)],
        out_specs=pl.BlockSpec((tb, SLAB, 128), lambda i: (i, 0, 0)),
        out_shape=jax.ShapeDtypeStruct((T, SLAB, 128), F32),
        scratch_shapes=[stage] * 4,
        compiler_params=pltpu.CompilerParams(dimension_semantics=("arbitrary",),
                                             vmem_limit_bytes=VMEM_LIMIT),
        name="peer_out",
    )(ids, c8, tab)


def _final_body(h_ref, p_ref, gain_ref, o_ref):
    x = h_ref[...] + p_ref[...]
    o_ref[...] = x * lax.rsqrt(jnp.mean(x * x, axis=-1, keepdims=True) + EPS) * gain_ref[...]


def _final(h, p, gain, tt):
    T = h.shape[0]
    row = lambda i: (i, 0)
    return pl.pallas_call(
        _final_body,
        grid=(T // tt,),
        in_specs=[pl.BlockSpec((tt, D_MODEL), row), pl.BlockSpec((tt, D_MODEL), row), _full((1, D_MODEL))],
        out_specs=pl.BlockSpec((tt, D_MODEL), row),
        out_shape=jax.ShapeDtypeStruct((T, D_MODEL), F32),
        compiler_params=pltpu.CompilerParams(dimension_semantics=("arbitrary",)),
        name="final_norm",
    )(h, p, gain)


def _tile(n, pref):
    t = min(n, pref)
    assert n % t == 0, (n, t)
    return t


def kernel(x, norm_mix, w_in, b_gate, mu_rwkv, w_lora_up, w0, a_lora_up, a0, g_lora_up, k_k, k_a, r_k, ln_x_w, ln_x_b, w_o_rwkv, s5_log_dt, s5_a_re, s5_a_im, s5_b_re, s5_b_im, s5_c_re, s5_c_im, s5_d, w_glu_s5, w_out, norm_ffn, peer_wq, peer_subkeys, peer_u, peer_v, norm_final):
    B, S, D = x.shape
    assert D == D_MODEL and w_in.shape[0] == 1 and S % CHUNK == 0
    T = B * S
    x2 = x.reshape(T, D)
    r1 = lambda z: z.reshape(1, -1)

    tt = _tile(S, 256)
    pr, ps, g = _inproj(x2, r1(norm_mix[0]), w_in[0].astype(BF16), r1(b_gate[0]), r1(mu_rwkv[0]), B, S // tt, tt)

    zpad = jnp.zeros((64, RW), F32)
    wup = jnp.concatenate([w_lora_up[0], zpad], axis=0)
    aup = jnp.concatenate([zpad, a_lora_up[0]], axis=0)
    hid = jnp.arange(RW) // HD
    bd = (hid[:, None] == hid[None, :]).astype(BF16)
    tc = _tile(S, 256)
    ya = _rwkv(pr, wup, r1(w0[0]), aup, r1(a0[0]), g_lora_up[0], r1(k_k[0]), r1(k_a[0]), r1(r_k[0]),
               r1(ln_x_w[0]), r1(ln_x_b[0]), bd, w_o_rwkv[0].astype(BF16), B, S // tc, tc)

    ts = _tile(S, 256)
    lpr, lpi, bbr, bbi = _s5_prep(s5_log_dt[0], s5_a_re[0], s5_a_im[0], s5_b_re[0], s5_b_im[0], ts)
    ng, ns = s5_a_re[0].shape
    hh = s5_b_re.shape[-1]
    nblk, gb = 4, ng // 4
    eye_g = jnp.eye(gb, dtype=F32)

    def in_blocks(bb):
        return jnp.einsum("jgph,gk->jghkp", bb.reshape(nblk, gb, ns, hh), eye_g).reshape(nblk, gb * hh, gb * ns)

    def out_blocks(cc):
        return jnp.einsum("jghp,gk->jgpkh", cc.reshape(nblk, gb, hh, ns), eye_g).reshape(nblk, gb * ns, gb * hh)

    bw = jnp.concatenate([in_blocks(bbr), in_blocks(bbi)], axis=2)
    cw = jnp.concatenate([out_blocks(s5_c_re[0]), -out_blocks(s5_c_im[0])], axis=1)
    yb = _s5(ps, jnp.stack(_split(bw)), jnp.stack(_split(cw)), r1(s5_d[0]), lpr, lpi, w_glu_s5[0].astype(BF16),
             B, S // ts, ts)

    _, flat = _stage2_layout()
    flat = jnp.asarray(flat, I32).reshape(-1, 1)
    keys = peer_subkeys[0].reshape(PEER_HEADS * 2, PEER_KEYS, PEER_HALF)
    tm = _tile(T, 256)
    h, xn, ids_t, gate_t = _mix(x2, ya, yb, g, w_out[0].astype(BF16), r1(norm_ffn[0]), peer_wq[0],
                                keys, flat, tm)
    ids = ids_t.T
    gate = gate_t.T

    tb = _tile(T, 256)
    c = _peer_hidden(ids, xn.reshape(T, SLAB, 128), gate, _table(peer_u[0]), tb)
    po = _peer_out(ids, c, _table(peer_v[0]), tb).reshape(T, D)
    out = _final(h, po, r1(norm_final), _tile(T, 512))
    return out.reshape(B, S, D)
```

```python
import functools
import math

import jax
import jax.numpy as jnp
from jax import lax
from jax.experimental import pallas as pl
from jax.experimental.pallas import tpu as pltpu

F32 = jnp.float32
BF16 = jnp.bfloat16
I32 = jnp.int32
HI = lax.Precision.HIGHEST

EPS = 1e-6
GN_EPS = 64e-5
D_MODEL = 1024
RW = 512
NH = 8
HD = 64
N_RWKV_COLS = 1792
S5W = 512
S5_STATES = 2048
PEER_HEADS = 8
PEER_KEYS = 128
PEER_TOPK = 16
PEER_HALF = 64
CHUNK = 64

VMEM_LIMIT = 56 * 1024 * 1024


def _dot(a, b, prec=None):
    return jnp.dot(a, b, preferred_element_type=F32, precision=prec)


def _dot_nt(a, b, prec=HI):
    return lax.dot_general(a, b, (((1,), (1,)), ((), ())), preferred_element_type=F32, precision=prec)


_NN = (((1,), (0,)), ((), ()))
_NT = (((1,), (1,)), ((), ()))
_TN = (((0,), (0,)), ((), ()))


def _split(x):
    hi = x.astype(BF16)
    return hi, (x - hi.astype(F32)).astype(BF16)


def _dot3(a, b, dims=_NN):
    dg = lambda p, q: lax.dot_general(p, q, dims, preferred_element_type=F32)
    return dg(a[0], b[0]) + (dg(a[0], b[1]) + dg(a[1], b[0]))


def _dot2(a, b, dims=_NN):
    dg = lambda p, q: lax.dot_general(p, q, dims, preferred_element_type=F32)
    return dg(a[0], b) + dg(a[1], b)


def _sigmoid(x):
    return 1.0 / (1.0 + jnp.exp(-x))


def _softplus(x):
    return jnp.maximum(x, 0.0) + jnp.log(1.0 + jnp.exp(-jnp.abs(x)))


def _gelu(x):
    return 0.5 * x * (1.0 + jnp.tanh(math.sqrt(2.0 / math.pi) * (x + 0.044715 * (x * x * x))))


def _full(shape):
    n = len(shape)
    return pl.BlockSpec(shape, lambda *_: (0,) * n)


def _inproj_body(x_ref, gain_ref, w_ref, bg_ref, mu_ref, pr_ref, ps_ref, g_ref, prev_ref):
    @pl.when(pl.program_id(1) == 0)
    def _():
        prev_ref[...] = jnp.zeros_like(prev_ref)

    x = x_ref[...]
    xn = x * lax.rsqrt(jnp.mean(x * x, axis=-1, keepdims=True) + EPS) * gain_ref[...]
    xb = xn.astype(BF16)
    p = _dot(xb, w_ref[:, :N_RWKV_COLS])
    tt = p.shape[0]
    row = lax.broadcasted_iota(I32, p.shape, 0)
    prev = jnp.broadcast_to(prev_ref[0:1, :], p.shape)
    shifted = jnp.where(row == 0, prev, pltpu.roll(p, 1, 0))
    prev_ref[0:1, :] = p[tt - 1:tt, :]
    pr_ref[...] = p + (shifted - p) * mu_ref[...]
    ps_ref[...] = _dot(xb, w_ref[:, N_RWKV_COLS:N_RWKV_COLS + S5W])
    g_ref[...] = _sigmoid(_dot(xb, w_ref[:, N_RWKV_COLS + S5W:]) + bg_ref[...])


def _inproj(x2, gain, w_in_bf, b_gate, mu, nb, tiles, tt):
    T = x2.shape[0]
    ncols = w_in_bf.shape[1]
    row = lambda b, t: (b * tiles + t, 0)
    return pl.pallas_call(
        _inproj_body,
        grid=(nb, tiles),
        in_specs=[pl.BlockSpec((tt, D_MODEL), row), _full((1, D_MODEL)), _full((D_MODEL, ncols)),
                  _full((1, 2 * D_MODEL)), _full((1, N_RWKV_COLS))],
        out_specs=[pl.BlockSpec((tt, N_RWKV_COLS), row), pl.BlockSpec((tt, S5W), row),
                   pl.BlockSpec((tt, 2 * D_MODEL), row)],
        out_shape=[jax.ShapeDtypeStruct((T, N_RWKV_COLS), F32), jax.ShapeDtypeStruct((T, S5W), F32),
                   jax.ShapeDtypeStruct((T, 2 * D_MODEL), F32)],
        scratch_shapes=[pltpu.VMEM((8, N_RWKV_COLS), F32)],
        compiler_params=pltpu.CompilerParams(dimension_semantics=("arbitrary", "arbitrary"),
                                             vmem_limit_bytes=VMEM_LIMIT),
        name="inproj",
    )(x2, gain, w_in_bf, b_gate, mu)


def _rwkv_body(pr_ref, wup_ref, w0_ref, aup_ref, a0_ref, gup_ref, kk_ref, ka_ref, rk_ref, lnw_ref, lnb_ref,
               bd_ref, wo_ref, ya_ref, st_ref, y_scr, *, nc):
    @pl.when(pl.program_id(1) == 0)
    def _():
        st_ref[...] = jnp.zeros_like(st_ref)

    L = CHUNK
    r = pr_ref[:, 0:RW]
    k = pr_ref[:, RW:2 * RW]
    v = pr_ref[:, 2 * RW:3 * RW]
    x128 = pr_ref[:, 3 * RW:3 * RW + 128]
    xg = pr_ref[:, 3 * RW + 128:3 * RW + 256]
    bd = bd_ref[...]
    seg = lambda z: _dot2(_split(z), bd)

    w_log = -_softplus(-(w0_ref[...] + _dot(jnp.tanh(x128), wup_ref[...], HI))) - 0.5
    logw = -jnp.exp(w_log)
    a_lr = _sigmoid(a0_ref[...] + _dot(x128, aup_ref[...], HI))
    g = _dot(_sigmoid(xg), gup_ref[...], HI)
    kk = k * kk_ref[...]
    k2 = k * (1.0 + (a_lr - 1.0) * ka_ref[...])
    kk = kk / jnp.maximum(jnp.sqrt(seg(kk * kk)), 1e-12)
    b = kk * a_lr
    am = -kk

    ri = lax.broadcasted_iota(I32, (L, L), 0)
    ci = lax.broadcasted_iota(I32, (L, L), 1)
    strict = ri > ci
    incl = ri >= ci
    eye = (ri == ci).astype(F32)
    tri = incl.astype(BF16)
    quad = []
    bit = 0
    while (1 << bit) < L:
        s = 1 << bit
        quad.append((((ri >> (bit + 1)) == (ci >> (bit + 1))) & ((ri & s) != 0) & ((ci & s) == 0)).astype(F32))
        bit += 1
    sl = [slice(h * HD, (h + 1) * HD) for h in range(NH)]
    hr = range(NH)

    m_all, cc_all, r2_all, y0_all = [], [], [], []
    for c_i in range(nc):
        rows = slice(c_i * L, (c_i + 1) * L)
        lw = logw[rows]
        lw_h = lw.astype(BF16)
        lw_r = lw - lw_h.astype(F32)
        lw_m = lw_r.astype(BF16)
        lw_l = (lw_r - lw_m.astype(F32)).astype(BF16)
        c = _dot(tri, lw_h) + (_dot(tri, lw_m) + _dot(tri, lw_l))
        cl = c[L - 1:L, :]
        e_c = jnp.exp(c)
        e_nc = jnp.exp(-c)
        e_cp = jnp.exp(c - lw)
        e_rem = jnp.exp(cl - c)
        e_last = jnp.exp(cl)
        at_all = am[rows] * e_cp
        rt_all = r[rows] * e_c
        bt_all = b[rows] * e_nc
        kt_all = k2[rows] * e_nc
        bh_all = b[rows] * e_rem
        kh_all = k2[rows] * e_rem
        v_all = v[rows]
        at = [at_all[:, s] for s in sl]
        rt = [rt_all[:, s] for s in sl]
        q = [_dot3(_split(jnp.concatenate([at[h], rt[h]], axis=0)),
                   _split(jnp.concatenate([bt_all[:, sl[h]], kt_all[:, sl[h]]], axis=0)), _NT) for h in hr]
        n_ab = [jnp.where(strict, q[h][:L, :L], 0.0) for h in hr]
        a_kr = [_split(jnp.concatenate([jnp.where(strict, q[h][:L, L:], 0.0),
                                        jnp.where(incl, q[h][L:, L:], 0.0)], axis=0)) for h in hr]
        a_rb = [_split(jnp.where(incl, q[h][L:, :L], 0.0)) for h in hr]
        dinv = [eye + n_ab[h] * quad[0] for h in hr]
        for qm in quad[1:]:
            ds = [_split(dinv[h]) for h in hr]
            t1 = [_dot3(_split(n_ab[h] * qm), ds[h]) for h in hr]
            dinv = [dinv[h] + _dot3(ds[h], _split(t1[h])) for h in hr]
        ds = [_split(dinv[h]) for h in hr]
        vs = [_split(v_all[:, s]) for s in sl]
        av = [_dot3(a_kr[h], vs[h]) for h in hr]
        at2 = [_dot3(ds[h], _split(at[h])) for h in hr]
        u0 = [_dot3(ds[h], _split(av[h][:L])) for h in hr]
        at2s = [_split(at2[h]) for h in hr]
        u0s = [_split(u0[h]) for h in hr]
        bhs = [_split(bh_all[:, s]) for s in sl]
        khs = [_split(kh_all[:, s]) for s in sl]
        m_all.append([eye * e_last[:, sl[h]] + _dot3(_split(at2[h].T), bhs[h]) for h in hr])
        cc_all.append([_dot3(_split(u0[h].T), bhs[h]) + _dot3(_split(v_all[:, sl[h]].T), khs[h]) for h in hr])
        r2_all.append([_split(rt[h] + _dot3(a_rb[h], at2s[h])) for h in hr])
        y0_all.append([_dot3(a_rb[h], u0s[h]) + av[h][L:] for h in hr])

    st = [st_ref[h] for h in hr]
    for c_i in range(nc):
        rows = slice(c_i * L, (c_i + 1) * L)
        ss = [_split(st[h]) for h in hr]
        for h in hr:
            y_scr[rows, sl[h]] = y0_all[c_i][h] + _dot3(r2_all[c_i][h], ss[h], _NT)
        st = [_dot3(ss[h], _split(m_all[c_i][h])) + cc_all[c_i][h] for h in hr]
    for h in hr:
        st_ref[h] = st[h]

    y = y_scr[...]
    mean = seg(y) * (1.0 / HD)
    yc = y - mean
    var = seg(yc * yc) * (1.0 / HD)
    yn = yc * lax.rsqrt(var + GN_EPS) * lnw_ref[...] + lnb_ref[...]
    bonus = seg(r * k2 * rk_ref[...]) * v
    out = (yn + bonus) * g
    ya_ref[...] = _dot(out.astype(BF16), wo_ref[...])


def _rwkv(pr, wup, w0, aup, a0, gup, k_k, k_a, r_k, ln_w, ln_b, bd, wo_bf, nb, tiles, tc):
    T = pr.shape[0]
    row = lambda b, t: (b * tiles + t, 0)
    return pl.pallas_call(
        functools.partial(_rwkv_body, nc=tc // CHUNK),
        grid=(nb, tiles),
        in_specs=[pl.BlockSpec((tc, N_RWKV_COLS), row), _full((128, RW)), _full((1, RW)), _full((128, RW)),
                  _full((1, RW)), _full((128, RW)), _full((1, RW)), _full((1, RW)), _full((1, RW)),
                  _full((1, RW)), _full((1, RW)), _full((RW, RW)), _full((RW, D_MODEL))],
        out_specs=pl.BlockSpec((tc, D_MODEL), row),
        out_shape=jax.ShapeDtypeStruct((T, D_MODEL), F32),
        scratch_shapes=[pltpu.VMEM((NH, HD, HD), F32), pltpu.VMEM((tc, RW), F32)],
        compiler_params=pltpu.CompilerParams(dimension_semantics=("arbitrary", "arbitrary"),
                                             vmem_limit_bytes=VMEM_LIMIT),
        name="rwkv",
    )(pr, wup, w0, aup, a0, gup, k_k, k_a, r_k, ln_w, ln_b, bd, wo_bf)


def _s5_prep_body(ldt_r, are_r, aim_r, ldt_c, are_c, aim_c, bre_ref, bim_ref, lpr_ref, lpi_ref, bbr_ref, bbi_ref, *, tc):
    dt = jnp.exp(ldt_r[...])
    mag = jnp.exp(are_r[...] * dt)
    ang = aim_r[...] * dt
    lpr_ref[0:1, :] = mag * jnp.cos(ang)
    lpi_ref[0:1, :] = mag * jnp.sin(ang)
    n = 1
    while n < tc:
        pr = lpr_ref[n - 1:n, :]
        pi = lpi_ref[n - 1:n, :]
        qr = lpr_ref[0:n, :]
        qi = lpi_ref[0:n, :]
        lpr_ref[n:2 * n, :] = qr * pr - qi * pi
        lpi_ref[n:2 * n, :] = qr * pi + qi * pr
        n *= 2
    dtc = jnp.exp(ldt_c[...])
    are = are_c[...]
    aim = aim_c[...]
    magc = jnp.exp(are * dtc)
    angc = aim * dtc
    lre = magc * jnp.cos(angc)
    lim = magc * jnp.sin(angc)
    den = are * are + aim * aim
    nre = lre - 1.0
    fre = (nre * are + lim * aim) / den
    fim = (lim * are - nre * aim) / den
    bre = bre_ref[...]
    bim = bim_ref[...]
    bbr_ref[...] = fre * bre - fim * bim
    bbi_ref[...] = fre * bim + fim * bre


def _s5_prep(log_dt, a_re, a_im, b_re, b_im, tc):
    ng, ns = a_re.shape
    hh = b_re.shape[-1]
    n = ng * ns
    ldt = jnp.broadcast_to(log_dt[:, None], (ng, ns))
    row = lambda z: z.reshape(1, n)
    col = lambda z: z.reshape(n, 1)
    return pl.pallas_call(
        functools.partial(_s5_prep_body, tc=tc),
        out_shape=[jax.ShapeDtypeStruct((tc, n), F32), jax.ShapeDtypeStruct((tc, n), F32),
                   jax.ShapeDtypeStruct((n, hh), F32), jax.ShapeDtypeStruct((n, hh), F32)],
        name="s5_prep",
    )(row(ldt), row(a_re), row(a_im), col(ldt), col(a_re), col(a_im), b_re.reshape(n, hh), b_im.reshape(n, hh))


def _s5_body(u_ref, bw_ref, cw_ref, d_ref, lpr_ref, lpi_ref, wglu_ref, yb_ref, st_ref):
    @pl.when(pl.program_id(1) == 0)
    def _():
        st_ref[...] = jnp.zeros_like(st_ref)

    u = u_ref[...]
    tc = u.shape[0]
    nblk = bw_ref.shape[1]
    sb = S5_STATES // nblk
    ub = S5W // nblk
    bus = [_dot3(_split(u[:, j * ub:(j + 1) * ub]), (bw_ref[0, j], bw_ref[1, j])) for j in range(nblk)]
    xr = jnp.concatenate([bu[:, :sb] for bu in bus], axis=1)
    xi = jnp.concatenate([bu[:, sb:] for bu in bus], axis=1)
    row = lax.broadcasted_iota(I32, (tc, 1), 0)
    d = 1
    while d < tc:
        lr = lpr_ref[d - 1:d, :]
        li = lpi_ref[d - 1:d, :]
        keep = row >= d
        sr = jnp.where(keep, pltpu.roll(xr, d, 0), 0.0)
        si = jnp.where(keep, pltpu.roll(xi, d, 0), 0.0)
        xr, xi = xr + lr * sr - li * si, xi + lr * si + li * sr
        d *= 2
    pr = st_ref[0:1, :]
    pi = st_ref[1:2, :]
    lr = lpr_ref[...]
    li = lpi_ref[...]
    xr, xi = xr + lr * pr - li * pi, xi + lr * pi + li * pr
    st_ref[0:1, :] = xr[tc - 1:tc, :]
    st_ref[1:2, :] = xi[tc - 1:tc, :]
    ys = [_dot3(_split(jnp.concatenate([xr[:, j * sb:(j + 1) * sb], xi[:, j * sb:(j + 1) * sb]], axis=1)),
                (cw_ref[0, j], cw_ref[1, j])) for j in range(nblk)]
    y = jnp.concatenate(ys, axis=1) + d_ref[...] * u
    z = _dot(_gelu(y).astype(BF16), wglu_ref[...])
    yb_ref[...] = z[:, :D_MODEL] * _sigmoid(z[:, D_MODEL:])


def _s5(ps, bw, cw, dd, lpr, lpi, wglu_bf, nb, tiles, tc):
    T = ps.shape[0]
    row = lambda b, t: (b * tiles + t, 0)
    return pl.pallas_call(
        _s5_body,
        grid=(nb, tiles),
        in_specs=[pl.BlockSpec((tc, S5W), row), _full(bw.shape), _full(cw.shape), _full((1, S5W)),
                  _full((tc, S5_STATES)), _full((tc, S5_STATES)), _full((S5W, 2 * D_MODEL))],
        out_specs=pl.BlockSpec((tc, D_MODEL), row),
        out_shape=jax.ShapeDtypeStruct((T, D_MODEL), F32),
        scratch_shapes=[pltpu.VMEM((8, S5_STATES), F32)],
        compiler_params=pltpu.CompilerParams(dimension_semantics=("arbitrary", "arbitrary"),
                                             vmem_limit_bytes=VMEM_LIMIT),
        name="s5",
    )(ps, bw, cw, dd, lpr, lpi, wglu_bf)


def _stage2_layout():
    blocks = [("row", 0, 16), ("row", 1, 8), ("row", 2, 8), ("row", 3, 8),
              ("col", 0, 16), ("col", 1, 8), ("col", 2, 8)]
    flat = []
    for kind, a, n in blocks:
        for m in range(n):
            i, j = (a, m) if kind == "row" else (m, a)
            ok = (i + 1) * (j + 1) <= PEER_TOPK and (kind == "row" or i >= 4)
            flat.append(i * PEER_TOPK + j if ok else -1)
    return blocks, flat


def _topk_rows(vals, key, n_out, big):
    out_v, out_k = [], []
    for _ in range(n_out):
        m = jnp.max(vals, axis=0, keepdims=True)
        sel = jnp.min(jnp.where(vals == m, key, big), axis=0, keepdims=True)
        out_v.append(m)
        out_k.append(sel)
        vals = jnp.where(key == sel, -jnp.inf, vals)
    return out_v, out_k


def _mix_body(x_ref, ya_ref, yb_ref, g_ref, wout_ref, nf_ref, wq_ref, keys_ref, flat_ref,
              h_ref, xn_ref, ids_ref, gate_ref, sv_scr, si_scr):
    g = g_ref[...]
    mixed = g[:, :D_MODEL] * ya_ref[...] + g[:, D_MODEL:] * yb_ref[...]
    h = x_ref[...] + _dot(mixed.astype(BF16), wout_ref[...])
    h_ref[...] = h
    xn = h * lax.rsqrt(jnp.mean(h * h, axis=-1, keepdims=True) + EPS) * nf_ref[...]
    xn_ref[...] = xn
    q = _dot(xn, wq_ref[...], HI)
    tt = q.shape[0]
    key_iota = lax.broadcasted_iota(I32, (PEER_KEYS, tt), 0).astype(F32)
    blocks, _ = _stage2_layout()
    flat = flat_ref[...]
    flat_b = jnp.broadcast_to(flat, (flat.shape[0], tt)).astype(F32)
    for hd in range(PEER_HEADS):
        for half in range(2):
            col = (hd * 2 + half) * PEER_HALF
            s_t = _dot_nt(keys_ref[hd * 2 + half], q[:, col:col + PEER_HALF])
            vs, ks = _topk_rows(s_t, key_iota, PEER_TOPK, float(PEER_KEYS))
            for i in range(PEER_TOPK):
                sv_scr[half, i:i + 1, :] = vs[i]
                si_scr[half, i:i + 1, :] = ks[i].astype(I32)
        sv1, sv2 = sv_scr[0], sv_scr[1]
        si1, si2 = si_scr[0], si_scr[1]
        cand, eid = [], []
        for kind, a, n in blocks:
            if kind == "row":
                cand.append(sv1[a:a + 1, :] + sv2[0:n, :])
                eid.append(si1[a:a + 1, :] * PEER_KEYS + si2[0:n, :])
            else:
                cand.append(sv1[0:n, :] + sv2[a:a + 1, :])
                eid.append(si1[0:n, :] * PEER_KEYS + si2[a:a + 1, :])
        cand = jnp.where(flat_b >= 0, jnp.concatenate(cand, axis=0), -jnp.inf)
        eid = jnp.concatenate(eid, axis=0) * (D_MODEL // 256)
        cv, cf = _topk_rows(cand, flat_b, PEER_TOPK, float(PEER_TOPK * PEER_TOPK))
        ex = [jnp.exp(c - cv[0]) for c in cv]
        tot = ex[0]
        for e in ex[1:]:
            tot = tot + e
        for i in range(PEER_TOPK):
            r = hd * PEER_TOPK + i
            ids_ref[r:r + 1, :] = jnp.sum(jnp.where(flat_b == cf[i], eid, 0), axis=0, keepdims=True)
            gate_ref[r:r + 1, :] = ex[i] / tot


def _mix(x2, ya, yb, g, wout_bf, norm_ffn, wq_bf, keys, flat, tt):
    T = x2.shape[0]
    row = lambda i: (i, 0)
    colb = lambda i: (0, i)
    nk = PEER_HEADS * PEER_TOPK
    return pl.pallas_call(
        _mix_body,
        grid=(T // tt,),
        in_specs=[pl.BlockSpec((tt, D_MODEL), row), pl.BlockSpec((tt, D_MODEL), row),
                  pl.BlockSpec((tt, D_MODEL), row), pl.BlockSpec((tt, 2 * D_MODEL), row),
                  _full((D_MODEL, D_MODEL)), _full((1, D_MODEL)), _full((D_MODEL, D_MODEL)),
                  _full(keys.shape), _full(flat.shape)],
        out_specs=[pl.BlockSpec((tt, D_MODEL), row), pl.BlockSpec((tt, D_MODEL), row),
                   pl.BlockSpec((nk, tt), colb), pl.BlockSpec((nk, tt), colb)],
        out_shape=[jax.ShapeDtypeStruct((T, D_MODEL), F32), jax.ShapeDtypeStruct((T, D_MODEL), F32),
                   jax.ShapeDtypeStruct((nk, T), I32), jax.ShapeDtypeStruct((nk, T), F32)],
        scratch_shapes=[pltpu.VMEM((2, PEER_TOPK, tt), F32), pltpu.VMEM((2, PEER_TOPK, tt), I32)],
        compiler_params=pltpu.CompilerParams(dimension_semantics=("arbitrary",),
                                             vmem_limit_bytes=VMEM_LIMIT),
        name="mix_topk",
    )(x2, ya, yb, g, wout_bf, norm_ffn, wq_bf, keys, flat)


NK = PEER_HEADS * PEER_TOPK
SLAB = D_MODEL // 128
GROUP = 8


def _table(tab):
    return tab.astype(BF16).reshape(tab.shape[0] * SLAB // 2, 2, 128)


def _slab_diag():
    lane = lax.broadcasted_iota(I32, (SLAB, NK * SLAB), 1)
    sub = lax.broadcasted_iota(I32, (SLAB, NK * SLAB), 0)
    return sub == (lane & (SLAB - 1))


def _gather_token(ids_ref, tab_ref, t, g_scr):
    ids_t = ids_ref.at[t]
    for mi in range(NK):
        e4 = pl.multiple_of(ids_t[mi], SLAB // 2)
        g_scr[mi * SLAB:(mi + 1) * SLAB, :] = tab_ref[pl.ds(e4, SLAB // 2)].reshape(SLAB, 128)


def _peer_hidden_body(ids_ref, x_ref, gate_ref, tab_ref, pool_ref, c_ref, ga_scr, gb_scr, s_scr):
    tb = x_ref.shape[0]
    diag = _slab_diag()

    def group(gi, carry):
        base = pl.multiple_of(gi * GROUP, GROUP)
        for i in range(GROUP):
            t = base + i
            g_scr = (ga_scr, gb_scr)[i % 2]
            _gather_token(ids_ref, tab_ref, t, g_scr)
            x3 = x_ref[t]
            r = _dot3_lhs(x3, g_scr[...], _NT)
            s_scr[i:i + 1, :] = jnp.sum(jnp.where(diag, r, 0.0), axis=0, keepdims=True)
        s8 = s_scr[...]
        s_hi = s8.astype(BF16)
        s_r = s8 - s_hi.astype(F32)
        s_mid = s_r.astype(BF16)
        s_lo = (s_r - s_mid.astype(F32)).astype(BF16)
        pool = pool_ref[...]
        hid = _dot(s_hi, pool) + (_dot(s_mid, pool) + _dot(s_lo, pool))
        rows = pl.ds(base, GROUP)
        c_ref[rows, :] = gate_ref[rows, :] * _gelu(hid)
        return carry

    lax.fori_loop(0, tb // GROUP, group, 0)


def _dot3_lhs(a, b_bf, dims):
    m = a.shape[0]
    a_hi = a.astype(BF16)
    a_lo = (a - a_hi.astype(F32)).astype(BF16)
    r = lax.dot_general(jnp.concatenate([a_hi, a_lo], axis=0), b_bf, dims, preferred_element_type=F32)
    return r[:m] + r[m:]


def _peer_hidden(ids, x3, gate, tab, tb):
    T = ids.shape[0]
    row = lambda i: (i, 0)
    pool = (jnp.arange(NK * SLAB)[:, None] // SLAB == jnp.arange(NK)[None, :]).astype(BF16)
    stage = pltpu.VMEM((NK * SLAB, 128), BF16)
    return pl.pallas_call(
        _peer_hidden_body,
        grid=(T // tb,),
        in_specs=[pl.BlockSpec((tb, NK), row, memory_space=pltpu.SMEM),
                  pl.BlockSpec((tb, SLAB, 128), lambda i: (i, 0, 0)), pl.BlockSpec((tb, NK), row),
                  _full(tab.shape), _full(pool.shape)],
        out_specs=pl.BlockSpec((tb, NK), row),
        out_shape=jax.ShapeDtypeStruct((T, NK), F32),
        scratch_shapes=[stage, stage, pltpu.VMEM((GROUP, NK * SLAB), F32)],
        compiler_params=pltpu.CompilerParams(dimension_semantics=("arbitrary",),
                                             vmem_limit_bytes=VMEM_LIMIT),
        name="peer_hidden",
    )(ids, x3, gate, tab, pool)


def _peer_out_body(ids_ref, c8_ref, tab_ref, o_ref, ga_scr, gb_scr, gc_scr, gd_scr):
    tb = o_ref.shape[0]
    diag = _slab_diag()

    def gather(t, g_scr):
        _gather_token(ids_ref, tab_ref, t, g_scr)

    def combine(t, g_scr):
        c8 = jnp.broadcast_to(c8_ref[pl.ds(t, 1), :], (SLAB, NK * SLAB))
        o_ref[t] = _dot3_lhs(jnp.where(diag, c8, 0.0), g_scr[...], _NN)

    def quad(q, carry):
        t0 = 4 * q
        combine(t0, ga_scr)
        combine(t0 + 1, gb_scr)
        gather(t0 + 2, gc_scr)
        gather(t0 + 3, gd_scr)
        combine(t0 + 2, gc_scr)
        combine(t0 + 3, gd_scr)
        t4 = jnp.minimum(t0 + 4, tb - 2)
        gather(t4, ga_scr)
        gather(t4 + 1, gb_scr)
        return carry

    gather(0, ga_scr)
    gather(1, gb_scr)
    lax.fori_loop(0, tb // 4, quad, 0)


def _peer_out(ids, c, tab, tb):
    T = ids.shape[0]
    row = lambda i: (i, 0)
    c8 = jnp.repeat(c, SLAB, axis=1)
    stage = pltpu.VMEM((NK * SLAB, 128), BF16)
    return pl.pallas_call(
        _peer_out_body,
        grid=(T // tb,),
        in_specs=[pl.BlockSpec((tb, NK), row, memory_space=pltpu.SMEM),
                  pl.BlockSpec((tb, NK * SLAB), row), _full(tab.shape)],
        out_specs=pl.BlockSpec((tb, SLAB, 128), lambda i: (i, 0, 0)),
        out_shape=jax.ShapeDtypeStruct((T, SLAB, 128), F32),
        scratch_shapes=[stage] * 4,
        compiler_params=pltpu.CompilerParams(dimension_semantics=("arbitrary",),
                                             vmem_limit_bytes=VMEM_LIMIT),
        name="peer_out",
    )(ids, c8, tab)


def _final_body(h_ref, p_ref, gain_ref, o_ref):
    x = h_ref[...] + p_ref[...]
    o_ref[...] = x * lax.rsqrt(jnp.mean(x * x, axis=-1, keepdims=True) + EPS) * gain_ref[...]


def _final(h, p, gain, tt):
    T = h.shape[0]
    row = lambda i: (i, 0)
    return pl.pallas_call(
        _final_body,
        grid=(T // tt,),
        in_specs=[pl.BlockSpec((tt, D_MODEL), row), pl.BlockSpec((tt, D_MODEL), row), _full((1, D_MODEL))],
        out_specs=pl.BlockSpec((tt, D_MODEL), row),
        out_shape=jax.ShapeDtypeStruct((T, D_MODEL), F32),
        compiler_params=pltpu.CompilerParams(dimension_semantics=("arbitrary",)),
        name="final_norm",
    )(h, p, gain)


def _tile(n, pref):
    t = min(n, pref)
    assert n % t == 0, (n, t)
    return t


def kernel(x, norm_mix, w_in, b_gate, mu_rwkv, w_lora_up, w0, a_lora_up, a0, g_lora_up, k_k, k_a, r_k, ln_x_w, ln_x_b, w_o_rwkv, s5_log_dt, s5_a_re, s5_a_im, s5_b_re, s5_b_im, s5_c_re, s5_c_im, s5_d, w_glu_s5, w_out, norm_ffn, peer_wq, peer_subkeys, peer_u, peer_v, norm_final):
    B, S, D = x.shape
    assert D == D_MODEL and w_in.shape[0] == 1 and S % CHUNK == 0
    T = B * S
    x2 = x.reshape(T, D)
    r1 = lambda z: z.reshape(1, -1)

    tt = _tile(S, 256)
    pr, ps, g = _inproj(x2, r1(norm_mix[0]), w_in[0].astype(BF16), r1(b_gate[0]), r1(mu_rwkv[0]), B, S // tt, tt)

    zpad = jnp.zeros((64, RW), F32)
    wup = jnp.concatenate([w_lora_up[0], zpad], axis=0)
    aup = jnp.concatenate([zpad, a_lora_up[0]], axis=0)
    hid = jnp.arange(RW) // HD
    bd = (hid[:, None] == hid[None, :]).astype(BF16)
    tc = _tile(S, 256)
    ya = _rwkv(pr, wup, r1(w0[0]), aup, r1(a0[0]), g_lora_up[0], r1(k_k[0]), r1(k_a[0]), r1(r_k[0]),
               r1(ln_x_w[0]), r1(ln_x_b[0]), bd, w_o_rwkv[0].astype(BF16), B, S // tc, tc)

    ts = _tile(S, 256)
    lpr, lpi, bbr, bbi = _s5_prep(s5_log_dt[0], s5_a_re[0], s5_a_im[0], s5_b_re[0], s5_b_im[0], ts)
    ng, ns = s5_a_re[0].shape
    hh = s5_b_re.shape[-1]
    nblk, gb = 4, ng // 4
    eye_g = jnp.eye(gb, dtype=F32)

    def in_blocks(bb):
        return jnp.einsum("jgph,gk->jghkp", bb.reshape(nblk, gb, ns, hh), eye_g).reshape(nblk, gb * hh, gb * ns)

    def out_blocks(cc):
        return jnp.einsum("jghp,gk->jgpkh", cc.reshape(nblk, gb, hh, ns), eye_g).reshape(nblk, gb * ns, gb * hh)

    bw = jnp.concatenate([in_blocks(bbr), in_blocks(bbi)], axis=2)
    cw = jnp.concatenate([out_blocks(s5_c_re[0]), -out_blocks(s5_c_im[0])], axis=1)
    yb = _s5(ps, jnp.stack(_split(bw)), jnp.stack(_split(cw)), r1(s5_d[0]), lpr, lpi, w_glu_s5[0].astype(BF16),
             B, S // ts, ts)

    _, flat = _stage2_layout()
    flat = jnp.asarray(flat, I32).reshape(-1, 1)
    keys = peer_subkeys[0].reshape(PEER_HEADS * 2, PEER_KEYS, PEER_HALF)
    tm = _tile(T, 256)
    h, xn, ids_t, gate_t = _mix(x2, ya, yb, g, w_out[0].astype(BF16), r1(norm_ffn[0]), peer_wq[0],
                                keys, flat, tm)
    ids = ids_t.T
    gate = gate_t.T

    tb = _tile(T, 256)
    c = _peer_hidden(ids, xn.reshape(T, SLAB, 128), gate, _table(peer_u[0]), tb)
    po = _peer_out(ids, c, _table(peer_v[0]), tb).reshape(T, D)
    out = _final(h, po, r1(norm_final), _tile(T, 512))
    return out.reshape(B, S, D)
```
